```python
import jax, jax.numpy as jnp
from jax import lax
import numpy as np

D_MODEL = 1024
BATCH = 8
SEQ = 2048
DEPTH = 2
DEC_BATCH = 128
DEC_SEQ = 1
PAST_LEN = 16384
PAGE_SIZE = 128

D_A = D_MODEL
D_B = D_MODEL
D_C = D_MODEL
K_A = 3
K_B = 31
CHUNK = 128
C_GROUPS = 8
C_HEAD = D_C // C_GROUPS
N_BRANCH = 3
SPLIT_WIDTHS = (D_A, D_A, D_A, D_B, D_B, D_C, D_C, N_BRANCH * D_MODEL)
D_IN_TOTAL = 3 * D_A + 2 * D_B + 2 * D_C + N_BRANCH * D_MODEL
N_EXPERTS = 16
N_EXPERT_GROUPS = 4
EXPERTS_PER_GROUP = N_EXPERTS // N_EXPERT_GROUPS
TOP_K = 2
D_EXPERT = D_MODEL // 2
ALPHA = (2 * DEPTH) ** 0.25
BETA = (8 * DEPTH) ** -0.25
LN_EPS = 1e-5

kernel_name = 'parallel_conv_chunkmlp_moe_decode_step'


def layer_norm(x, g, b):
    xf = x.astype(jnp.float32)
    mu = jnp.mean(xf, axis=-1, keepdims=True)
    xc = xf - mu
    var = jnp.mean(xc * xc, axis=-1, keepdims=True)
    return (xc * lax.rsqrt(var + LN_EPS) * g.astype(jnp.float32) + b.astype(jnp.float32)).astype(x.dtype)


def causal_dwconv(x_full, w):
    c = x_full.shape[-1]
    return lax.conv_general_dilated(x_full, w[:, None, :].astype(x_full.dtype), window_strides=(1,), padding='VALID',
                                    dimension_numbers=('NWC', 'WIO', 'NWC'), feature_group_count=c)


def chunk_spatial_gate(v, w_s, b_s):
    n, L, _ = v.shape
    ch = min(L, CHUNK)
    nc = -(-L // ch)
    lp = nc * ch
    if lp != L:
        v = jnp.pad(v, ((0, 0), (0, lp - L), (0, 0)))
    causal = jnp.tril(jnp.ones((ch, ch), dtype=bool))
    w = jnp.where(causal[None], w_s[:, :ch, :ch], 0).astype(v.dtype)
    vc = v.reshape(n, nc, ch, C_GROUPS, C_HEAD)
    s = jnp.einsum('gts,bcsgd->bctgd', w, vc) + jnp.swapaxes(b_s[:, :ch], 0, 1)[None, None, :, :, None].astype(v.dtype)
    return s.reshape(n, lp, D_C)[:, :L]


def token_mixer(x, ctx_a, ctx_b, w_in, conv_a_w, w_a_out, conv_b_w, conv_b_bias, norm_b_g, norm_b_b, w_b_out,
                v_norm_g, v_norm_b, w_spatial, b_spatial, w_c_out, w_o):
    z = jnp.einsum('bld,de->ble', x, w_in)
    a_bg, a_cg, a_h, b_val, b_gate, c_u, c_v, gate_pre = jnp.split(z, list(np.cumsum(SPLIT_WIDTHS)[:-1]), axis=-1)
    a_full = jnp.concatenate([ctx_a, a_cg * a_h], axis=1)
    y_a = jnp.einsum('blc,cd->bld', a_bg * causal_dwconv(a_full, conv_a_w), w_a_out)
    b_full = jnp.concatenate([ctx_b, b_val * jax.nn.sigmoid(b_gate)], axis=1)
    b_conv = causal_dwconv(b_full, conv_b_w) + conv_b_bias
    y_b = jnp.einsum('blc,cd->bld', jax.nn.silu(layer_norm(b_conv, norm_b_g, norm_b_b)), w_b_out)
    v = layer_norm(c_v, v_norm_g, v_norm_b)
    y_c = jnp.einsum('blc,cd->bld', c_u * chunk_spatial_gate(v, w_spatial, b_spatial), w_c_out)
    g_a, g_b, g_c = jnp.split(jax.nn.sigmoid(gate_pre), N_BRANCH, axis=-1)
    merged = g_a * y_a + g_b * y_b + g_c * y_c
    out = jnp.einsum('bld,de->ble', merged, w_o)
    return out, a_full[:, -(K_A - 1):], b_full[:, -(K_B - 1):], v


def moe_ffn(h, router_w, router_b, w_gate, w_up, w_down):
    n, L, d = h.shape
    t = h.reshape(n * L, d)
    logits = jnp.dot(t.astype(jnp.float32), router_w.astype(jnp.float32)) + router_b.astype(jnp.float32)
    probs = jax.nn.softmax(logits, axis=-1)
    grouped = probs.reshape(-1, N_EXPERT_GROUPS, EXPERTS_PER_GROUP)
    top_v, top_i = lax.top_k(grouped, TOP_K)
    best = jnp.argmax(jnp.sum(top_v, axis=-1), axis=-1)
    sel_v = jnp.take_along_axis(top_v, best[:, None, None], axis=1)[:, 0]
    sel_i = jnp.take_along_axis(top_i, best[:, None, None], axis=1)[:, 0] + best[:, None] * EXPERTS_PER_GROUP
    weights = sel_v / jnp.sum(sel_v, axis=-1, keepdims=True)
    gates = jnp.sum(jax.nn.one_hot(sel_i, N_EXPERTS, dtype=jnp.float32) * weights[..., None], axis=1).astype(h.dtype)
    out = jnp.zeros_like(t)
    for e in range(N_EXPERTS):
        u = jax.nn.silu(t @ w_gate[e]) * (t @ w_up[e])
        out = out + gates[:, e:e + 1] * (u @ w_down[e])
    return out.reshape(n, L, d)


def decoder_layer(x, ctx_a, ctx_b, w_in, conv_a_w, w_a_out, conv_b_w, conv_b_bias, norm_b_g, norm_b_b, w_b_out,
                  v_norm_g, v_norm_b, w_spatial, b_spatial, w_c_out, w_o, ln1_g, ln1_b, router_w, router_b,
                  expert_w_gate, expert_w_up, expert_w_down, ln2_g, ln2_b):
    m, new_a, new_b, v = token_mixer(x, ctx_a, ctx_b, w_in, conv_a_w, w_a_out, conv_b_w, conv_b_bias, norm_b_g,
                                     norm_b_b, w_b_out, v_norm_g, v_norm_b, w_spatial, b_spatial, w_c_out, w_o)
    h = layer_norm(ALPHA * x + m, ln1_g, ln1_b)
    f = moe_ffn(h, router_w, router_b, expert_w_gate, expert_w_up, expert_w_down)
    y = layer_norm(ALPHA * h + f, ln2_g, ln2_b)
    return y, new_a, new_b, v


def setup_inputs(seed: int = 0) -> dict:
    key = jax.random.key(seed)
    ks = iter(jax.random.split(key, 40))
    f32 = jnp.float32

    def nrm(shape, scale):
        return jax.random.normal(next(ks), shape, f32) * scale

    return {
        'x_prompt': nrm((BATCH, SEQ, D_MODEL), 1.0),
        'x_sample': nrm((DEC_BATCH, DEC_SEQ, D_MODEL), 1.0),
        'state_conv_a': nrm((DEPTH, DEC_BATCH, K_A - 1, D_A), 1.0),
        'state_conv_b': nrm((DEPTH, DEC_BATCH, K_B - 1, D_B), 1.0),
        'w_in': nrm((DEPTH, D_MODEL, D_IN_TOTAL), D_MODEL ** -0.5),
        'conv_a_w': nrm((DEPTH, K_A, D_A), K_A ** -0.5),
        'w_a_out': nrm((DEPTH, D_A, D_MODEL), BETA * D_A ** -0.5),
        'conv_b_w': nrm((DEPTH, K_B, D_B), K_B ** -0.5),
        'conv_b_bias': nrm((DEPTH, D_B), 0.02),
        'norm_b_g': 1.0 + nrm((DEPTH, D_B), 0.1),
        'norm_b_b': nrm((DEPTH, D_B), 0.02),
        'w_b_out': nrm((DEPTH, D_B, D_MODEL), BETA * D_B ** -0.5),
        'v_norm_g': 1.0 + nrm((DEPTH, D_C), 0.1),
        'v_norm_b': nrm((DEPTH, D_C), 0.02),
        'w_spatial': nrm((DEPTH, C_GROUPS, CHUNK, CHUNK), CHUNK ** -0.5),
        'b_spatial': 1.0 + nrm((DEPTH, C_GROUPS, CHUNK), 0.1),
        'w_c_out': nrm((DEPTH, D_C, D_MODEL), BETA * D_C ** -0.5),
        'w_o': nrm((DEPTH, D_MODEL, D_MODEL), BETA * D_MODEL ** -0.5),
        'ln1_g': 1.0 + nrm((DEPTH, D_MODEL), 0.1),
        'ln1_b': nrm((DEPTH, D_MODEL), 0.02),
        'router_w': nrm((D_MODEL, N_EXPERTS), D_MODEL ** -0.5),
        'router_b': nrm((N_EXPERTS,), 0.01),
        'expert_w_gate': nrm((DEPTH, N_EXPERTS, D_MODEL, D_EXPERT), D_MODEL ** -0.5),
        'expert_w_up': nrm((DEPTH, N_EXPERTS, D_MODEL, D_EXPERT), D_MODEL ** -0.5),
        'expert_w_down': nrm((DEPTH, N_EXPERTS, D_EXPERT, D_MODEL), BETA * D_EXPERT ** -0.5),
        'ln2_g': 1.0 + nrm((DEPTH, D_MODEL), 0.1),
        'ln2_b': nrm((DEPTH, D_MODEL), 0.02),
    }


def reference(x_prompt, x_sample, state_conv_a, state_conv_b, w_in, conv_a_w, w_a_out, conv_b_w, conv_b_bias,
              norm_b_g, norm_b_b, w_b_out, v_norm_g, v_norm_b, w_spatial, b_spatial, w_c_out, w_o, ln1_g, ln1_b,
              router_w, router_b, expert_w_gate, expert_w_up, expert_w_down, ln2_g, ln2_b):
    xp, xs = x_prompt, x_sample
    pa, pb, sa, sb, sv = [], [], [], [], []
    zero_a = jnp.zeros((xp.shape[0], K_A - 1, D_A), xp.dtype)
    zero_b = jnp.zeros((xp.shape[0], K_B - 1, D_B), xp.dtype)
    for l in range(DEPTH):
        lp = dict(w_in=w_in[l], conv_a_w=conv_a_w[l], w_a_out=w_a_out[l], conv_b_w=conv_b_w[l],
                  conv_b_bias=conv_b_bias[l], norm_b_g=norm_b_g[l], norm_b_b=norm_b_b[l], w_b_out=w_b_out[l],
                  v_norm_g=v_norm_g[l], v_norm_b=v_norm_b[l], w_spatial=w_spatial[l], b_spatial=b_spatial[l],
                  w_c_out=w_c_out[l], w_o=w_o[l], ln1_g=ln1_g[l], ln1_b=ln1_b[l], router_w=router_w,
                  router_b=router_b, expert_w_gate=expert_w_gate[l], expert_w_up=expert_w_up[l],
                  expert_w_down=expert_w_down[l], ln2_g=ln2_g[l], ln2_b=ln2_b[l])
        xp, na, nb, _ = decoder_layer(xp, zero_a, zero_b, **lp)
        pa.append(na)
        pb.append(nb)
        xs, na, nb, v = decoder_layer(xs, state_conv_a[l], state_conv_b[l], **lp)
        sa.append(na)
        sb.append(nb)
        sv.append(v)
    return (xp, xs, jnp.stack(pa), jnp.stack(pb), jnp.stack(sa), jnp.stack(sb), jnp.stack(sv))
```

```python
import functools

import jax
import jax.numpy as jnp
from jax import lax
from jax.experimental import pallas as pl
from jax.experimental.pallas import tpu as pltpu

F32 = jnp.float32
BF16 = jnp.bfloat16

K_A = 3
K_B = 31
CHUNK = 128
C_GROUPS = 8
N_EXPERTS = 16
N_EXPERT_GROUPS = 4
EXPERTS_PER_GROUP = N_EXPERTS // N_EXPERT_GROUPS
LN_EPS = 1e-5

SUBLANES = 8
A_PAD = SUBLANES
B_PAD = 4 * SUBLANES
CONV_ROWS = 16
MIX_TM = 256
MOE_TM = 1024
VMEM_LIMIT = 56 * 1024 * 1024


def _sigmoid(x):
    return 1.0 / (1.0 + jnp.exp(-x))


def _layer_norm(x, g, b):
    mu = jnp.mean(x, axis=-1, keepdims=True)
    xc = x - mu
    var = jnp.mean(xc * xc, axis=-1, keepdims=True)
    return xc * lax.rsqrt(var + LN_EPS) * g + b


def _dot(a, b):
    return jnp.dot(a, b, preferred_element_type=F32)


def _first_max(vals):
    m = functools.reduce(jnp.maximum, vals)
    hits, taken = [], None
    for v in vals:
        hit = v == m
        if taken is None:
            taken = hit
        else:
            hit = jnp.logical_and(hit, jnp.logical_not(taken))
            taken = jnp.logical_or(taken, hit)
        hits.append(hit)
    return m, hits


def _router_gates_t(h, rw_t, rb, gt_ref):
    logits = lax.dot_general(rw_t, h, (((1,), (1,)), ((), ())), precision=lax.Precision.HIGHEST,
                             preferred_element_type=F32) + rb
    p = jnp.exp(logits - jnp.max(logits, axis=0, keepdims=True))
    rows = [p[e:e + 1, :] for e in range(N_EXPERTS)]
    scores, selected = [], []
    for g in range(N_EXPERT_GROUPS):
        a = rows[g * EXPERTS_PER_GROUP:(g + 1) * EXPERTS_PER_GROUP]
        m1, is1 = _first_max(a)
        rest = [jnp.where(hit, -1.0, v) for hit, v in zip(is1, a)]
        m2, is2 = _first_max(rest)
        scores.append(m1 + m2)
        selected.append([jnp.logical_or(x, y) for x, y in zip(is1, is2)])
    _, best = _first_max(scores)
    for g in range(N_EXPERT_GROUPS):
        inv = 1.0 / scores[g]
        for j in range(EXPERTS_PER_GROUP):
            e = g * EXPERTS_PER_GROUP + j
            keep = jnp.logical_and(best[g], selected[g][j])
            gt_ref[e:e + 1, :] = jnp.where(keep, rows[e] * inv, 0.0)


def _merge_and_norm(x, zg, ya, yb, yc, w_o_ref, ln_g_ref, ln_b_ref, alpha):
    d = x.shape[-1]
    merged = (_sigmoid(zg[:, :d]) * ya + _sigmoid(zg[:, d:2 * d]) * yb + _sigmoid(zg[:, 2 * d:]) * yc)
    m = _dot(merged.astype(BF16), w_o_ref[...])
    return _layer_norm(alpha * x + m, ln_g_ref[...], ln_b_ref[...])


def _mixer_prompt_kernel(x_ref, w_in_ref, ca_w_ref, w_a_out_ref, cb_w_ref, cb_bias_ref, nb_g_ref, nb_b_ref,
                         w_b_out_ref, vn_g_ref, vn_b_ref, w_sp_ref, b_sp_t_ref, w_c_out_ref, w_o_ref,
                         ln_g_ref, ln_b_ref, rw_t_ref, rb_ref,
                         h_ref, gt_ref, new_a_ref, new_b_ref,
                         a_ext, b_ext, cb_buf, c_buf, *, alpha):
    tm, d = x_ref.shape
    i = pl.program_id(1)

    @pl.when(i == 0)
    def _():
        a_ext[0:A_PAD, :] = jnp.zeros((A_PAD, d), F32)
        b_ext[0:B_PAD, :] = jnp.zeros((B_PAD, d), F32)

    x = x_ref[...]
    xb = x.astype(BF16)

    za = _dot(xb, w_in_ref[:, 0:3 * d])
    a_in = za[:, d:2 * d] * za[:, 2 * d:3 * d]
    a_ext[A_PAD:A_PAD + tm, :] = a_in
    conv_a = (ca_w_ref[0:1, :] * a_ext[A_PAD - 2:A_PAD - 2 + tm, :]
              + ca_w_ref[1:2, :] * a_ext[A_PAD - 1:A_PAD - 1 + tm, :]
              + ca_w_ref[2:3, :] * a_in)
    new_a_ref[...] = a_ext[A_PAD + tm - (K_A - 1):A_PAD + tm, :]
    a_ext[0:A_PAD, :] = a_ext[tm:tm + A_PAD, :]
    ya = _dot((za[:, 0:d] * conv_a).astype(BF16), w_a_out_ref[...])

    zb = _dot(xb, w_in_ref[:, 3 * d:5 * d])
    b_ext[B_PAD:B_PAD + tm, :] = zb[:, 0:d] * _sigmoid(zb[:, d:2 * d])
    first = B_PAD - (K_B - 1)
    for r0 in range(0, tm, CONV_ROWS):
        acc = cb_w_ref[0:1, :] * b_ext[first + r0:first + r0 + CONV_ROWS, :]
        for k in range(1, K_B):
            acc = acc + cb_w_ref[k:k + 1, :] * b_ext[first + r0 + k:first + r0 + k + CONV_ROWS, :]
        cb_buf[r0:r0 + CONV_ROWS, :] = acc
    new_b_ref[...] = b_ext[B_PAD + tm - (K_B - 1):B_PAD + tm, :]
    b_ext[0:B_PAD, :] = b_ext[tm:tm + B_PAD, :]
    b_n = _layer_norm(cb_buf[...] + cb_bias_ref[...], nb_g_ref[...], nb_b_ref[...])
    yb = _dot((b_n * _sigmoid(b_n)).astype(BF16), w_b_out_ref[...])

    zc = _dot(xb, w_in_ref[:, 5 * d:7 * d])
    v = _layer_norm(zc[:, d:2 * d], vn_g_ref[...], vn_b_ref[...]).astype(BF16)
    row = lax.broadcasted_iota(jnp.int32, (CHUNK, CHUNK), 0)
    col = lax.broadcasted_iota(jnp.int32, (CHUNK, CHUNK), 1)
    c_head = d // C_GROUPS
    for g in range(C_GROUPS):
        w_g = jnp.where(col <= row, w_sp_ref[g], 0.0).astype(BF16)
        bias_g = b_sp_t_ref[:, g:g + 1]
        for c in range(tm // CHUNK):
            s = _dot(w_g, v[c * CHUNK:(c + 1) * CHUNK, g * c_head:(g + 1) * c_head]) + bias_g
            c_buf[c * CHUNK:(c + 1) * CHUNK, g * c_head:(g + 1) * c_head] = s
    yc = _dot((zc[:, 0:d] * c_buf[...]).astype(BF16), w_c_out_ref[...])

    zg = _dot(xb, w_in_ref[:, 7 * d:10 * d])
    h = _merge_and_norm(x, zg, ya, yb, yc, w_o_ref, ln_g_ref, ln_b_ref, alpha)
    h_ref[...] = h
    _router_gates_t(h, rw_t_ref[...], rb_ref[...], gt_ref)


def _mixer_sample_kernel(x_ref, ctx_a_ref, ctx_b_ref, w_in_ref, ca_w_ref, w_a_out_ref, cb_w_ref, cb_bias_ref,
                         nb_g_ref, nb_b_ref, w_b_out_ref, vn_g_ref, vn_b_ref, sp_scale_ref, sp_bias_ref,
                         w_c_out_ref, w_o_ref, ln_g_ref, ln_b_ref, rw_t_ref, rb_ref,
                         h_ref, gt_ref, a_in_ref, b_glu_ref, v_ref, *, alpha):
    d = x_ref.shape[-1]
    x = x_ref[...]
    xb = x.astype(BF16)

    za = _dot(xb, w_in_ref[:, 0:3 * d])
    a_in = za[:, d:2 * d] * za[:, 2 * d:3 * d]
    conv_a = ca_w_ref[K_A - 1:K_A, :] * a_in
    for k in range(K_A - 1):
        conv_a = conv_a + ca_w_ref[k:k + 1, :] * ctx_a_ref[k]
    a_in_ref[...] = a_in
    ya = _dot((za[:, 0:d] * conv_a).astype(BF16), w_a_out_ref[...])

    zb = _dot(xb, w_in_ref[:, 3 * d:5 * d])
    b_glu = zb[:, 0:d] * _sigmoid(zb[:, d:2 * d])
    conv_b = cb_w_ref[K_B - 1:K_B, :] * b_glu
    for k in range(K_B - 1):
        conv_b = conv_b + cb_w_ref[k:k + 1, :] * ctx_b_ref[k]
    b_glu_ref[...] = b_glu
    b_n = _layer_norm(conv_b + cb_bias_ref[...], nb_g_ref[...], nb_b_ref[...])
    yb = _dot((b_n * _sigmoid(b_n)).astype(BF16), w_b_out_ref[...])

    zc = _dot(xb, w_in_ref[:, 5 * d:7 * d])
    v = _layer_norm(zc[:, d:2 * d], vn_g_ref[...], vn_b_ref[...])
    v_ref[...] = v
    yc = _dot((zc[:, 0:d] * (v * sp_scale_ref[...] + sp_bias_ref[...])).astype(BF16), w_c_out_ref[...])

    zg = _dot(xb, w_in_ref[:, 7 * d:10 * d])
    h = _merge_and_norm(x, zg, ya, yb, yc, w_o_ref, ln_g_ref, ln_b_ref, alpha)
    h_ref[...] = h
    _router_gates_t(h, rw_t_ref[...], rb_ref[...], gt_ref)


def _moe_dense_kernel(h_ref, g_ref, wg_ref, wu_ref, wd_ref, ln_g_ref, ln_b_ref, out_ref, acc_ref, hb_ref, *,
                      alpha):
    e = pl.program_id(1)

    @pl.when(e == 0)
    def _():
        hb_ref[...] = h_ref[...].astype(BF16)
        acc_ref[...] = jnp.zeros_like(acc_ref)

    hb = hb_ref[...]
    gate = _dot(hb, wg_ref[...])
    u = gate * _sigmoid(gate) * _dot(hb, wu_ref[...])
    y = _dot(u.astype(BF16), wd_ref[...])
    g = g_ref[...]
    lane = lax.broadcasted_iota(jnp.int32, g.shape, 1)
    g_e = jnp.sum(jnp.where(lane == e, g, 0.0), axis=-1, keepdims=True)
    acc_ref[...] += g_e * y

    @pl.when(e == pl.num_programs(1) - 1)
    def _():
        out_ref[...] = _layer_norm(alpha * h_ref[...] + acc_ref[...], ln_g_ref[...], ln_b_ref[...])


def _resident(shape):
    zeros = (0,) * len(shape)
    return pl.BlockSpec(shape, lambda *_: zeros, pipeline_mode=pl.Buffered(1))


def _mixer_prompt(x, p, alpha):
    n_seq, seq, d = x.shape
    tm = MIX_TM
    assert seq % tm == 0 and tm % CHUNK == 0 and seq >= K_B - 1
    n_i = seq // tm
    weights = (p['w_in'], p['conv_a_w'], p['w_a_out'], p['conv_b_w'], p['conv_b_bias'], p['norm_b_g'],
               p['norm_b_b'], p['w_b_out'], p['v_norm_g'], p['v_norm_b'], p['w_spatial'], p['b_spatial_t'],
               p['w_c_out'], p['w_o'], p['ln1_g'], p['ln1_b'], p['router_w_t'], p['router_b'])
    return pl.pallas_call(
        functools.partial(_mixer_prompt_kernel, alpha=alpha),
        grid=(n_seq, n_i),
        in_specs=[pl.BlockSpec((None, tm, d), lambda b, i: (b, i, 0))] + [_resident(w.shape) for w in weights],
        out_specs=[pl.BlockSpec((None, tm, d), lambda b, i: (b, i, 0)),
                   pl.BlockSpec((N_EXPERTS, tm), lambda b, i: (0, b * n_i + i)),
                   pl.BlockSpec((None, K_A - 1, d), lambda b, i: (b, 0, 0)),
                   pl.BlockSpec((None, K_B - 1, d), lambda b, i: (b, 0, 0))],
        out_shape=[jax.ShapeDtypeStruct((n_seq, seq, d), F32),
                   jax.ShapeDtypeStruct((N_EXPERTS, n_seq * seq), F32),
                   jax.ShapeDtypeStruct((n_seq, K_A - 1, d), F32),
                   jax.ShapeDtypeStruct((n_seq, K_B - 1, d), F32)],
        scratch_shapes=[pltpu.VMEM((A_PAD + tm, d), F32), pltpu.VMEM((B_PAD + tm, d), F32),
                        pltpu.VMEM((tm, d), F32), pltpu.VMEM((tm, d), F32)],
        compiler_params=pltpu.CompilerParams(dimension_semantics=("arbitrary", "arbitrary"),
                                             vmem_limit_bytes=VMEM_LIMIT),
        name="mixer_prompt",
    )(x, *weights)


def _mixer_sample(x, ctx_a_t, ctx_b_t, p, alpha):
    n, d = x.shape
    weights = (p['w_in'], p['conv_a_w'], p['w_a_out'], p['conv_b_w'], p['conv_b_bias'], p['norm_b_g'],
               p['norm_b_b'], p['w_b_out'], p['v_norm_g'], p['v_norm_b'], p['sp_scale'], p['sp_bias'],
               p['w_c_out'], p['w_o'], p['ln1_g'], p['ln1_b'], p['router_w_t'], p['router_b'])
    return pl.pallas_call(
        functools.partial(_mixer_sample_kernel, alpha=alpha),
        out_shape=[jax.ShapeDtypeStruct((n, d), F32),
                   jax.ShapeDtypeStruct((N_EXPERTS, n), F32),
                   jax.ShapeDtypeStruct((n, d), F32),
                   jax.ShapeDtypeStruct((n, d), F32),
                   jax.ShapeDtypeStruct((n, d), F32)],
        compiler_params=pltpu.CompilerParams(vmem_limit_bytes=VMEM_LIMIT),
        name="mixer_sample",
    )(x, ctx_a_t, ctx_b_t, *weights)


def _moe_dense(h, gates, p, alpha, tm):
    t, d = h.shape
    assert t % tm == 0
    d_e = p['expert_w_gate'].shape[-1]
    return pl.pallas_call(
        functools.partial(_moe_dense_kernel, alpha=alpha),
        grid=(t // tm, N_EXPERTS),
        in_specs=[pl.BlockSpec((tm, d), lambda i, e: (i, 0)),
                  pl.BlockSpec((tm, N_EXPERTS), lambda i, e: (i, 0)),
                  pl.BlockSpec((None, d, d_e), lambda i, e: (e, 0, 0)),
                  pl.BlockSpec((None, d, d_e), lambda i, e: (e, 0, 0)),
                  pl.BlockSpec((None, d_e, d), lambda i, e: (e, 0, 0)),
                  pl.BlockSpec((1, d), lambda i, e: (0, 0)),
                  pl.BlockSpec((1, d), lambda i, e: (0, 0))],
        out_specs=pl.BlockSpec((tm, d), lambda i, e: (i, 0)),
        out_shape=jax.ShapeDtypeStruct((t, d), F32),
        scratch_shapes=[pltpu.VMEM((tm, d), F32), pltpu.VMEM((tm, d), BF16)],
        compiler_params=pltpu.CompilerParams(dimension_semantics=("arbitrary", "arbitrary"),
                                             vmem_limit_bytes=VMEM_LIMIT),
        name="moe_dense",
    )(h, gates, p['expert_w_gate'], p['expert_w_up'], p['expert_w_down'], p['ln2_g'], p['ln2_b'])


def kernel(x_prompt, x_sample, state_conv_a, state_conv_b, w_in, conv_a_w, w_a_out, conv_b_w, conv_b_bias, norm_b_g, norm_b_b, w_b_out, v_norm_g, v_norm_b, w_spatial, b_spatial, w_c_out, w_o, ln1_g, ln1_b, router_w, router_b, expert_w_gate, expert_w_up, expert_w_down, ln2_g, ln2_b):
    depth = w_in.shape[0]
    n_seq, seq, d = x_prompt.shape
    n_dec, dec_seq, _ = x_sample.shape
    assert dec_seq == 1
    alpha = (2 * depth) ** 0.25
    c_head = d // C_GROUPS

    xp = x_prompt
    xs = x_sample.reshape(n_dec, d)
    router_w_t = router_w.T
    router_b_col = router_b.reshape(N_EXPERTS, 1)
    pa, pb, sa, sb, sv = [], [], [], [], []
    for l in range(depth):
        row = lambda a: a[l].reshape(1, d)
        p = dict(w_in=w_in[l].astype(BF16), conv_a_w=conv_a_w[l], w_a_out=w_a_out[l].astype(BF16),
                 conv_b_w=conv_b_w[l], conv_b_bias=row(conv_b_bias), norm_b_g=row(norm_b_g),
                 norm_b_b=row(norm_b_b), w_b_out=w_b_out[l].astype(BF16), v_norm_g=row(v_norm_g),
                 v_norm_b=row(v_norm_b), w_spatial=w_spatial[l], b_spatial_t=b_spatial[l].T,
                 sp_scale=jnp.repeat(w_spatial[l, :, 0, 0], c_head).reshape(1, d),
                 sp_bias=jnp.repeat(b_spatial[l, :, 0], c_head).reshape(1, d),
                 w_c_out=w_c_out[l].astype(BF16), w_o=w_o[l].astype(BF16), ln1_g=row(ln1_g), ln1_b=row(ln1_b),
                 router_w_t=router_w_t, router_b=router_b_col,
                 expert_w_gate=expert_w_gate[l].astype(BF16), expert_w_up=expert_w_up[l].astype(BF16),
                 expert_w_down=expert_w_down[l].astype(BF16), ln2_g=row(ln2_g), ln2_b=row(ln2_b))

        hp, gtp, na, nb = _mixer_prompt(xp, p, alpha)
        pa.append(na)
        pb.append(nb)
        xp = _moe_dense(hp.reshape(n_seq * seq, d), gtp.T, p, alpha, MOE_TM).reshape(n_seq, seq, d)

        hs, gts, a_in, b_glu, v = _mixer_sample(xs, jnp.swapaxes(state_conv_a[l], 0, 1),
                                                jnp.swapaxes(state_conv_b[l], 0, 1), p, alpha)
        sa.append(jnp.concatenate([state_conv_a[l][:, 1:], a_in[:, None, :]], axis=1))
        sb.append(jnp.concatenate([state_conv_b[l][:, 1:], b_glu[:, None, :]], axis=1))
        sv.append(v.reshape(n_dec, 1, d))
        xs = _moe_dense(hs, gts.T, p, alpha, n_dec)

    return (xp, xs.reshape(n_dec, 1, d), jnp.stack(pa), jnp.stack(pb), jnp.stack(sa), jnp.stack(sb),
            jnp.stack(sv))
```

```python
import functools

import jax
import jax.numpy as jnp
from jax import lax
from jax.experimental import pallas as pl
from jax.experimental.pallas import tpu as pltpu

F32 = jnp.float32
BF16 = jnp.bfloat16

K_A = 3
K_B = 31
CHUNK = 128
C_GROUPS = 8
N_EXPERTS = 16
N_EXPERT_GROUPS = 4
EXPERTS_PER_GROUP = N_EXPERTS // N_EXPERT_GROUPS
LN_EPS = 1e-5

SUBLANES = 8
A_PAD = SUBLANES
CONV_GROUPS = 4
MIX_TM = 256
MOE_TM = 1024
VMEM_LIMIT = 56 * 1024 * 1024


def _sigmoid(x):
    return 1.0 / (1.0 + jnp.exp(-x))


def _layer_norm(x, g, b):
    mu = jnp.mean(x, axis=-1, keepdims=True)
    xc = x - mu
    var = jnp.mean(xc * xc, axis=-1, keepdims=True)
    return xc * lax.rsqrt(var + LN_EPS) * g + b


def _dot(a, b):
    return jnp.dot(a, b, preferred_element_type=F32)


def _interleave_segments(a, seg):
    n, d = a.shape
    return jnp.swapaxes(a.reshape(n // seg, seg, d), 0, 1).reshape(n, d)


def _first_max(vals):
    m = functools.reduce(jnp.maximum, vals)
    hits, taken = [], None
    for v in vals:
        hit = v == m
        if taken is None:
            taken = hit
        else:
            hit = jnp.logical_and(hit, jnp.logical_not(taken))
            taken = jnp.logical_or(taken, hit)
        hits.append(hit)
    return m, hits


def _router_gates_t(h, rw_t, rb, gt_ref):
    logits = lax.dot_general(rw_t, h, (((1,), (1,)), ((), ())), precision=lax.Precision.HIGHEST,
                             preferred_element_type=F32) + rb
    p = jnp.exp(logits - jnp.max(logits, axis=0, keepdims=True))
    rows = [p[e:e + 1, :] for e in range(N_EXPERTS)]
    scores, selected = [], []
    for g in range(N_EXPERT_GROUPS):
        a = rows[g * EXPERTS_PER_GROUP:(g + 1) * EXPERTS_PER_GROUP]
        m1, is1 = _first_max(a)
        rest = [jnp.where(hit, -1.0, v) for hit, v in zip(is1, a)]
        m2, is2 = _first_max(rest)
        scores.append(m1 + m2)
        selected.append([jnp.logical_or(x, y) for x, y in zip(is1, is2)])
    _, best = _first_max(scores)
    for g in range(N_EXPERT_GROUPS):
        inv = 1.0 / scores[g]
        for j in range(EXPERTS_PER_GROUP):
            e = g * EXPERTS_PER_GROUP + j
            keep = jnp.logical_and(best[g], selected[g][j])
            gt_ref[e:e + 1, :] = jnp.where(keep, rows[e] * inv, 0.0)


def _merge_and_norm(x, zg, ya, yb, yc, w_o_ref, ln_g_ref, ln_b_ref, alpha):
    d = x.shape[-1]
    merged = (_sigmoid(zg[:, :d]) * ya + _sigmoid(zg[:, d:2 * d]) * yb + _sigmoid(zg[:, 2 * d:]) * yc)
    m = _dot(merged.astype(BF16), w_o_ref[...])
    return _layer_norm(alpha * x + m, ln_g_ref[...], ln_b_ref[...])


def _mixer_prompt_kernel(x_ref, xn_ref, w_in_ref, ca_w_ref, w_a_out_ref, cb_w_ref, cb_bias_ref, nb_g_ref,
                         nb_b_ref, w_b_out_ref, vn_g_ref, vn_b_ref, w_sp_ref, b_sp_t_ref, w_c_out_ref, w_o_ref,
                         ln_g_ref, ln_b_ref, rw_t_ref, rb_ref,
                         h_ref, gt_ref, new_a_ref, new_b_ref,
                         z_scr, a_ext, b_ext, b_perm, cb_buf, c_buf, *, alpha, tiles_per_seq):
    tm, d = x_ref.shape
    n = pl.program_id(0)

    @pl.when(n == 0)
    def _():
        xb0 = x_ref[...].astype(BF16)
        for c0, c1 in ((0, 3 * d), (3 * d, 5 * d), (5 * d, 7 * d), (7 * d, 10 * d)):
            z_scr[:, c0:c1] = _dot(xb0, w_in_ref[:, c0:c1])

    @pl.when(n % tiles_per_seq == 0)
    def _():
        a_ext[0:A_PAD, :] = jnp.zeros((A_PAD, d), F32)
        b_ext[0:tm // SUBLANES, :] = jnp.zeros((tm // SUBLANES, d), F32)

    x = x_ref[...]
    xnb = xn_ref[...].astype(BF16)

    a_in = z_scr[:, d:2 * d] * z_scr[:, 2 * d:3 * d]
    a_ext[A_PAD:A_PAD + tm, :] = a_in
    conv_a = (ca_w_ref[0:1, :] * a_ext[A_PAD - 2:A_PAD - 2 + tm, :]
              + ca_w_ref[1:2, :] * a_ext[A_PAD - 1:A_PAD - 1 + tm, :]
              + ca_w_ref[2:3, :] * a_in)
    new_a_ref[...] = a_ext[A_PAD + tm - (K_A - 1):A_PAD + tm, :]
    a_ext[0:A_PAD, :] = a_ext[tm:tm + A_PAD, :]
    a_pre = (z_scr[:, 0:d] * conv_a).astype(BF16)
    z_scr[:, 0:3 * d] = _dot(xnb, w_in_ref[:, 0:3 * d])
    ya = _dot(a_pre, w_a_out_ref[...])

    seg = tm // SUBLANES
    halo = (K_B - 1) * SUBLANES
    b_ext[seg:seg + tm, :] = z_scr[:, 3 * d:4 * d] * _sigmoid(z_scr[:, 4 * d:5 * d])
    z_scr[:, 3 * d:5 * d] = _dot(xnb, w_in_ref[:, 3 * d:5 * d])
    b_perm[0:halo, :] = _interleave_segments(b_ext[0:tm, :], seg)[tm - halo:, :]
    b_perm[halo:halo + tm, :] = _interleave_segments(b_ext[seg:seg + tm, :], seg)
    for r0 in range(0, tm, CONV_GROUPS * SUBLANES):
        accs = [None] * CONV_GROUPS
        for k in range(K_B):
            w_k = cb_w_ref[k]
            for u in range(CONV_GROUPS):
                r = SUBLANES * (k + u) + r0
                tap = w_k * b_perm[r:r + SUBLANES, :]
                accs[u] = tap if k == 0 else accs[u] + tap
        for u in range(CONV_GROUPS):
            cb_buf[r0 + SUBLANES * u:r0 + SUBLANES * (u + 1), :] = accs[u]
    new_b_ref[...] = b_ext[seg + tm - (K_B - 1):seg + tm, :]
    b_ext[0:seg, :] = b_ext[tm:tm + seg, :]
    conv_b = _interleave_segments(cb_buf[...], SUBLANES)
    b_n = _layer_norm(conv_b + cb_bias_ref[...], nb_g_ref[...], nb_b_ref[...])
    yb = _dot((b_n * _sigmoid(b_n)).astype(BF16), w_b_out_ref[...])

    v = _layer_norm(z_scr[:, 6 * d:7 * d], vn_g_ref[...], vn_b_ref[...]).astype(BF16)
    row = lax.broadcasted_iota(jnp.int32, (CHUNK, CHUNK), 0)
    col = lax.broadcasted_iota(jnp.int32, (CHUNK, CHUNK), 1)
    c_head = d // C_GROUPS
    for g in range(C_GROUPS):
        w_g = jnp.where(col <= row, w_sp_ref[g], 0.0).astype(BF16)
        bias_g = b_sp_t_ref[:, g:g + 1]
        for c in range(tm // CHUNK):
            s = _dot(w_g, v[c * CHUNK:(c + 1) * CHUNK, g * c_head:(g + 1) * c_head]) + bias_g
            c_buf[c * CHUNK:(c + 1) * CHUNK, g * c_head:(g + 1) * c_head] = s
    c_pre = (z_scr[:, 5 * d:6 * d] * c_buf[...]).astype(BF16)
    z_scr[:, 5 * d:7 * d] = _dot(xnb, w_in_ref[:, 5 * d:7 * d])
    yc = _dot(c_pre, w_c_out_ref[...])

    merged = (_sigmoid(z_scr[:, 7 * d:8 * d]) * ya + _sigmoid(z_scr[:, 8 * d:9 * d]) * yb
              + _sigmoid(z_scr[:, 9 * d:10 * d]) * yc).astype(BF16)
    z_scr[:, 7 * d:10 * d] = _dot(xnb, w_in_ref[:, 7 * d:10 * d])
    h = _layer_norm(alpha * x + _dot(merged, w_o_ref[...]), ln_g_ref[...], ln_b_ref[...])
    h_ref[...] = h
    _router_gates_t(h, rw_t_ref[...], rb_ref[...], gt_ref)


def _mixer_sample_kernel(x_ref, ctx_a_ref, ctx_b_ref, w_in_ref, ca_w_ref, w_a_out_ref, cb_w_ref, cb_bias_ref,
                         nb_g_ref, nb_b_ref, w_b_out_ref, vn_g_ref, vn_b_ref, sp_scale_ref, sp_bias_ref,
                         w_c_out_ref, w_o_ref, ln_g_ref, ln_b_ref, rw_t_ref, rb_ref,
                         h_ref, gt_ref, a_in_ref, b_glu_ref, v_ref, *, alpha):
    d = x_ref.shape[-1]
    x = x_ref[...]
    xb = x.astype(BF16)

    za = _dot(xb, w_in_ref[:, 0:3 * d])
    a_in = za[:, d:2 * d] * za[:, 2 * d:3 * d]
    conv_a = ca_w_ref[K_A - 1:K_A, :] * a_in
    for k in range(K_A - 1):
        conv_a = conv_a + ca_w_ref[k:k + 1, :] * ctx_a_ref[k]
    a_in_ref[...] = a_in
    ya = _dot((za[:, 0:d] * conv_a).astype(BF16), w_a_out_ref[...])

    zb = _dot(xb, w_in_ref[:, 3 * d:5 * d])
    b_glu = zb[:, 0:d] * _sigmoid(zb[:, d:2 * d])
    conv_b = cb_w_ref[K_B - 1:K_B, :] * b_glu
    for k in range(K_B - 1):
        conv_b = conv_b + cb_w_ref[k:k + 1, :] * ctx_b_ref[k]
    b_glu_ref[...] = b_glu
    b_n = _layer_norm(conv_b + cb_bias_ref[...], nb_g_ref[...], nb_b_ref[...])
    yb = _dot((b_n * _sigmoid(b_n)).astype(BF16), w_b_out_ref[...])

    zc = _dot(xb, w_in_ref[:, 5 * d:7 * d])
    v = _layer_norm(zc[:, d:2 * d], vn_g_ref[...], vn_b_ref[...])
    v_ref[...] = v
    yc = _dot((zc[:, 0:d] * (v * sp_scale_ref[...] + sp_bias_ref[...])).astype(BF16), w_c_out_ref[...])

    zg = _dot(xb, w_in_ref[:, 7 * d:10 * d])
    h = _merge_and_norm(x, zg, ya, yb, yc, w_o_ref, ln_g_ref, ln_b_ref, alpha)
    h_ref[...] = h
    _router_gates_t(h, rw_t_ref[...], rb_ref[...], gt_ref)


def _moe_dense_kernel(h_ref, g_ref, wg_ref, wu_ref, wd_ref, ln_g_ref, ln_b_ref, out_ref, acc_ref, hb_ref, *,
                      alpha):
    e = pl.program_id(1)

    @pl.when(e == 0)
    def _():
        hb_ref[...] = h_ref[...].astype(BF16)
        acc_ref[...] = jnp.zeros_like(acc_ref)

    hb = hb_ref[...]
    gate = _dot(hb, wg_ref[...])
    u = gate * _sigmoid(gate) * _dot(hb, wu_ref[...])
    y = _dot(u.astype(BF16), wd_ref[...])
    g = g_ref[...]
    lane = lax.broadcasted_iota(jnp.int32, g.shape, 1)
    g_e = jnp.sum(jnp.where(lane == e, g, 0.0), axis=-1, keepdims=True)
    acc_ref[...] += g_e * y

    @pl.when(e == pl.num_programs(1) - 1)
    def _():
        out_ref[...] = _layer_norm(alpha * h_ref[...] + acc_ref[...], ln_g_ref[...], ln_b_ref[...])


def _resident(shape):
    zeros = (0,) * len(shape)
    return pl.BlockSpec(shape, lambda *_: zeros, pipeline_mode=pl.Buffered(1))


def _mixer_prompt(x, p, alpha):
    n_seq, seq, d = x.shape
    tm = MIX_TM
    seg = tm // SUBLANES
    assert seq % tm == 0 and tm % CHUNK == 0 and seg % SUBLANES == 0 and seg >= K_B - 1
    n_i = seq // tm
    n_tiles = n_seq * n_i
    weights = (p['w_in'], p['conv_a_w'], p['w_a_out'], p['conv_b_w8'], p['conv_b_bias'], p['norm_b_g'],
               p['norm_b_b'], p['w_b_out'], p['v_norm_g'], p['v_norm_b'], p['w_spatial'], p['b_spatial_t'],
               p['w_c_out'], p['w_o'], p['ln1_g'], p['ln1_b'], p['router_w_t'], p['router_b'])
    x2 = x.reshape(n_seq * seq, d)
    h, gt, new_a, new_b = pl.pallas_call(
        functools.partial(_mixer_prompt_kernel, alpha=alpha, tiles_per_seq=n_i),
        grid=(n_tiles,),
        in_specs=[pl.BlockSpec((tm, d), lambda n: (n, 0)),
                  pl.BlockSpec((tm, d), lambda n: (jnp.minimum(n + 1, n_tiles - 1), 0))]
        + [_resident(w.shape) for w in weights],
        out_specs=[pl.BlockSpec((tm, d), lambda n: (n, 0)),
                   pl.BlockSpec((N_EXPERTS, tm), lambda n: (0, n)),
                   pl.BlockSpec((None, K_A - 1, d), lambda n: (n // n_i, 0, 0)),
                   pl.BlockSpec((None, K_B - 1, d), lambda n: (n // n_i, 0, 0))],
        out_shape=[jax.ShapeDtypeStruct((n_seq * seq, d), F32),
                   jax.ShapeDtypeStruct((N_EXPERTS, n_seq * seq), F32),
                   jax.ShapeDtypeStruct((n_seq, K_A - 1, d), F32),
                   jax.ShapeDtypeStruct((n_seq, K_B - 1, d), F32)],
        scratch_shapes=[pltpu.VMEM((tm, p['w_in'].shape[1]), F32),
                        pltpu.VMEM((A_PAD + tm, d), F32), pltpu.VMEM((seg + tm, d), F32),
                        pltpu.VMEM(((K_B - 1) * SUBLANES + tm, d), F32),
                        pltpu.VMEM((tm, d), F32), pltpu.VMEM((tm, d), F32)],
        compiler_params=pltpu.CompilerParams(dimension_semantics=("arbitrary",),
                                             vmem_limit_bytes=VMEM_LIMIT),
        name="mixer_prompt",
    )(x2, x2, *weights)
    return h, gt, new_a, new_b


def _mixer_sample(x, ctx_a_t, ctx_b_t, p, alpha):
    n, d = x.shape
    weights = (p['w_in'], p['conv_a_w'], p['w_a_out'], p['conv_b_w'], p['conv_b_bias'], p['norm_b_g'],
               p['norm_b_b'], p['w_b_out'], p['v_norm_g'], p['v_norm_b'], p['sp_scale'], p['sp_bias'],
               p['w_c_out'], p['w_o'], p['ln1_g'], p['ln1_b'], p['router_w_t'], p['router_b'])
    return pl.pallas_call(
        functools.partial(_mixer_sample_kernel, alpha=alpha),
        out_shape=[jax.ShapeDtypeStruct((n, d), F32),
                   jax.ShapeDtypeStruct((N_EXPERTS, n), F32),
                   jax.ShapeDtypeStruct((n, d), F32),
                   jax.ShapeDtypeStruct((n, d), F32),
                   jax.ShapeDtypeStruct((n, d), F32)],
        compiler_params=pltpu.CompilerParams(vmem_limit_bytes=VMEM_LIMIT),
        name="mixer_sample",
    )(x, ctx_a_t, ctx_b_t, *weights)


def _moe_dense(h, gates, p, alpha, tm):
    t, d = h.shape
    assert t % tm == 0
    d_e = p['expert_w_gate'].shape[-1]
    return pl.pallas_call(
        functools.partial(_moe_dense_kernel, alpha=alpha),
        grid=(t // tm, N_EXPERTS),
        in_specs=[pl.BlockSpec((tm, d), lambda i, e: (i, 0)),
                  pl.BlockSpec((tm, N_EXPERTS), lambda i, e: (i, 0)),
                  pl.BlockSpec((None, d, d_e), lambda i, e: (e, 0, 0)),
                  pl.BlockSpec((None, d, d_e), lambda i, e: (e, 0, 0)),
                  pl.BlockSpec((None, d_e, d), lambda i, e: (e, 0, 0)),
                  pl.BlockSpec((1, d), lambda i, e: (0, 0)),
                  pl.BlockSpec((1, d), lambda i, e: (0, 0))],
        out_specs=pl.BlockSpec((tm, d), lambda i, e: (i, 0)),
        out_shape=jax.ShapeDtypeStruct((t, d), F32),
        scratch_shapes=[pltpu.VMEM((tm, d), F32), pltpu.VMEM((tm, d), BF16)],
        compiler_params=pltpu.CompilerParams(dimension_semantics=("arbitrary", "arbitrary"),
                                             vmem_limit_bytes=VMEM_LIMIT),
        name="moe_dense",
    )(h, gates, p['expert_w_gate'], p['expert_w_up'], p['expert_w_down'], p['ln2_g'], p['ln2_b'])


def kernel(x_prompt, x_sample, state_conv_a, state_conv_b, w_in, conv_a_w, w_a_out, conv_b_w, conv_b_bias, norm_b_g, norm_b_b, w_b_out, v_norm_g, v_norm_b, w_spatial, b_spatial, w_c_out, w_o, ln1_g, ln1_b, router_w, router_b, expert_w_gate, expert_w_up, expert_w_down, ln2_g, ln2_b):
    depth = w_in.shape[0]
    n_seq, seq, d = x_prompt.shape
    n_dec, dec_seq, _ = x_sample.shape
    assert dec_seq == 1
    alpha = (2 * depth) ** 0.25
    c_head = d // C_GROUPS

    xp = x_prompt
    xs = x_sample.reshape(n_dec, d)
    router_w_t = router_w.T
    router_b_col = router_b.reshape(N_EXPERTS, 1)
    pa, pb, sa, sb, sv = [], [], [], [], []
    for l in range(depth):
        row = lambda a: a[l].reshape(1, d)
        p = dict(w_in=w_in[l].astype(BF16), conv_a_w=conv_a_w[l], w_a_out=w_a_out[l].astype(BF16),
                 conv_b_w=conv_b_w[l], conv_b_bias=row(conv_b_bias), norm_b_g=row(norm_b_g),
                 conv_b_w8=jnp.broadcast_to(conv_b_w[l][:, None, :], (K_B, SUBLANES, d)),
                 norm_b_b=row(norm_b_b), w_b_out=w_b_out[l].astype(BF16), v_norm_g=row(v_norm_g),
                 v_norm_b=row(v_norm_b), w_spatial=w_spatial[l], b_spatial_t=b_spatial[l].T,
                 sp_scale=jnp.repeat(w_spatial[l, :, 0, 0], c_head).reshape(1, d),
                 sp_bias=jnp.repeat(b_spatial[l, :, 0], c_head).reshape(1, d),
                 w_c_out=w_c_out[l].astype(BF16), w_o=w_o[l].astype(BF16), ln1_g=row(ln1_g), ln1_b=row(ln1_b),
                 router_w_t=router_w_t, router_b=router_b_col,
                 expert_w_gate=expert_w_gate[l].astype(BF16), expert_w_up=expert_w_up[l].astype(BF16),
                 expert_w_down=expert_w_down[l].astype(BF16), ln2_g=row(ln2_g), ln2_b=row(ln2_b))

        hp, gtp, na, nb = _mixer_prompt(xp, p, alpha)
        pa.append(na)
        pb.append(nb)
        xp = _moe_dense(hp, gtp.T, p, alpha, MOE_TM).reshape(n_seq, seq, d)

        hs, gts, a_in, b_glu, v = _mixer_sample(xs, jnp.swapaxes(state_conv_a[l], 0, 1),
                                                jnp.swapaxes(state_conv_b[l], 0, 1), p, alpha)
        sa.append(jnp.concatenate([state_conv_a[l][:, 1:], a_in[:, None, :]], axis=1))
        sb.append(jnp.concatenate([state_conv_b[l][:, 1:], b_glu[:, None, :]], axis=1))
        sv.append(v.reshape(n_dec, 1, d))
        xs = _moe_dense(hs, gts.T, p, alpha, n_dec)

    return (xp, xs.reshape(n_dec, 1, d), jnp.stack(pa), jnp.stack(pb), jnp.stack(sa), jnp.stack(sb),
            jnp.stack(sv))
```

```python
import functools

import jax
import jax.numpy as jnp
from jax import lax
from jax.experimental import pallas as pl
from jax.experimental.pallas import tpu as pltpu

F32 = jnp.float32
BF16 = jnp.bfloat16

K_A = 3
K_B = 31
CHUNK = 128
C_GROUPS = 8
N_EXPERTS = 16
N_EXPERT_GROUPS = 4
EXPERTS_PER_GROUP = N_EXPERTS // N_EXPERT_GROUPS
LN_EPS = 1e-5

SUBLANES = 8
LANES = 128
A_PAD = SUBLANES
CONV_GROUPS = 4
MIX_TM = 256
MOE_TM = 512
PERMUTE_ROWS = 2048
VMEM_LIMIT = 56 * 1024 * 1024


def _sigmoid(x):
    return 1.0 / (1.0 + jnp.exp(-x))


def _layer_norm(x, g, b):
    mu = jnp.mean(x, axis=-1, keepdims=True)
    xc = x - mu
    var = jnp.mean(xc * xc, axis=-1, keepdims=True)
    return xc * lax.rsqrt(var + LN_EPS) * g + b


def _dot(a, b):
    return jnp.dot(a, b, preferred_element_type=F32)


def _interleave_segments(a, seg):
    n, d = a.shape
    return jnp.swapaxes(a.reshape(n // seg, seg, d), 0, 1).reshape(n, d)


def _first_max(vals):
    m = functools.reduce(jnp.maximum, vals)
    hits, taken = [], None
    for v in vals:
        hit = v == m
        if taken is None:
            taken = hit
        else:
            hit = jnp.logical_and(hit, jnp.logical_not(taken))
            taken = jnp.logical_or(taken, hit)
        hits.append(hit)
    return m, hits


def _top2(vals):
    m1, is1 = _first_max(vals)
    rest = [jnp.where(hit, -1.0, v) for hit, v in zip(is1, vals)]
    m2, is2 = _first_max(rest)
    return m1 + m2, [jnp.logical_or(x, y) for x, y in zip(is1, is2)]


def _router_rows(h, rw_t, rb):
    logits = lax.dot_general(rw_t, h, (((1,), (1,)), ((), ())), precision=lax.Precision.HIGHEST,
                             preferred_element_type=F32) + rb
    p = jnp.exp(logits - jnp.max(logits, axis=0, keepdims=True))
    return [p[e:e + 1, :] for e in range(N_EXPERTS)]


def _group_top2(rows):
    picks = [_top2(rows[g * EXPERTS_PER_GROUP:(g + 1) * EXPERTS_PER_GROUP]) for g in range(N_EXPERT_GROUPS)]
    scores = [s for s, _ in picks]
    _, best = _first_max(scores)
    return scores, [sel for _, sel in picks], best


def _router_gates_t(h, rw_t, rb, gt_ref):
    rows = _router_rows(h, rw_t, rb)
    scores, selected, best = _group_top2(rows)
    for g in range(N_EXPERT_GROUPS):
        inv = 1.0 / scores[g]
        for j in range(EXPERTS_PER_GROUP):
            e = g * EXPERTS_PER_GROUP + j
            keep = jnp.logical_and(best[g], selected[g][j])
            gt_ref[e:e + 1, :] = jnp.where(keep, rows[e] * inv, 0.0)


def _router_group_t(h, rw_t, rb, grp_ref):
    _, _, best = _group_top2(_router_rows(h, rw_t, rb))
    grp = jnp.zeros(best[0].shape, jnp.int32)
    for g in range(1, N_EXPERT_GROUPS):
        grp = jnp.where(best[g], g, grp)
    grp_ref[...] = grp


def _merge_and_norm(x, zg, ya, yb, yc, w_o_ref, ln_g_ref, ln_b_ref, alpha):
    d = x.shape[-1]
    merged = (_sigmoid(zg[:, :d]) * ya + _sigmoid(zg[:, d:2 * d]) * yb + _sigmoid(zg[:, 2 * d:]) * yc)
    m = _dot(merged.astype(BF16), w_o_ref[...])
    return _layer_norm(alpha * x + m, ln_g_ref[...], ln_b_ref[...])


def _mixer_prompt_kernel(x_ref, xn_ref, w_in_ref, ca_w_ref, w_a_out_ref, cb_w_ref, cb_bias_ref, nb_g_ref,
                         nb_b_ref, w_b_out_ref, vn_g_ref, vn_b_ref, w_sp_ref, b_sp_t_ref, w_c_out_ref, w_o_ref,
                         ln_g_ref, ln_b_ref, rw_t_ref, rb_ref,
                         h_ref, grp_ref, new_a_ref, new_b_ref,
                         z_scr, a_ext, b_ext, b_perm, cb_buf, c_buf, *, alpha, tiles_per_seq):
    tm, d = x_ref.shape
    n = pl.program_id(0)

    @pl.when(n == 0)
    def _():
        xb0 = x_ref[...].astype(BF16)
        for c0, c1 in ((0, 3 * d), (3 * d, 5 * d), (5 * d, 7 * d), (7 * d, 10 * d)):
            z_scr[:, c0:c1] = _dot(xb0, w_in_ref[:, c0:c1])

    @pl.when(n % tiles_per_seq == 0)
    def _():
        a_ext[0:A_PAD, :] = jnp.zeros((A_PAD, d), F32)
        b_ext[0:tm // SUBLANES, :] = jnp.zeros((tm // SUBLANES, d), F32)

    x = x_ref[...]
    xnb = xn_ref[...].astype(BF16)

    a_in = z_scr[:, d:2 * d] * z_scr[:, 2 * d:3 * d]
    a_ext[A_PAD:A_PAD + tm, :] = a_in
    conv_a = (ca_w_ref[0:1, :] * a_ext[A_PAD - 2:A_PAD - 2 + tm, :]
              + ca_w_ref[1:2, :] * a_ext[A_PAD - 1:A_PAD - 1 + tm, :]
              + ca_w_ref[2:3, :] * a_in)
    new_a_ref[...] = a_ext[A_PAD + tm - (K_A - 1):A_PAD + tm, :]
    a_ext[0:A_PAD, :] = a_ext[tm:tm + A_PAD, :]
    a_pre = (z_scr[:, 0:d] * conv_a).astype(BF16)
    z_scr[:, 0:3 * d] = _dot(xnb, w_in_ref[:, 0:3 * d])
    ya = _dot(a_pre, w_a_out_ref[...])

    seg = tm // SUBLANES
    halo = (K_B - 1) * SUBLANES
    b_ext[seg:seg + tm, :] = z_scr[:, 3 * d:4 * d] * _sigmoid(z_scr[:, 4 * d:5 * d])
    z_scr[:, 3 * d:5 * d] = _dot(xnb, w_in_ref[:, 3 * d:5 * d])
    b_perm[0:halo, :] = _interleave_segments(b_ext[0:tm, :], seg)[tm - halo:, :]
    b_perm[halo:halo + tm, :] = _interleave_segments(b_ext[seg:seg + tm, :], seg)
    for r0 in range(0, tm, CONV_GROUPS * SUBLANES):
        accs = [None] * CONV_GROUPS
        for k in range(K_B):
            w_k = cb_w_ref[k]
            for u in range(CONV_GROUPS):
                r = SUBLANES * (k + u) + r0
                tap = w_k * b_perm[r:r + SUBLANES, :]
                accs[u] = tap if k == 0 else accs[u] + tap
        for u in range(CONV_GROUPS):
            cb_buf[r0 + SUBLANES * u:r0 + SUBLANES * (u + 1), :] = accs[u]
    new_b_ref[...] = b_ext[seg + tm - (K_B - 1):seg + tm, :]
    b_ext[0:seg, :] = b_ext[tm:tm + seg, :]
    conv_b = _interleave_segments(cb_buf[...], SUBLANES)
    b_n = _layer_norm(conv_b + cb_bias_ref[...], nb_g_ref[...], nb_b_ref[...])
    yb = _dot((b_n * _sigmoid(b_n)).astype(BF16), w_b_out_ref[...])

    v = _layer_norm(z_scr[:, 6 * d:7 * d], vn_g_ref[...], vn_b_ref[...]).astype(BF16)
    row = lax.broadcasted_iota(jnp.int32, (CHUNK, CHUNK), 0)
    col = lax.broadcasted_iota(jnp.int32, (CHUNK, CHUNK), 1)
    c_head = d // C_GROUPS
    for g in range(C_GROUPS):
        w_g = jnp.where(col <= row, w_sp_ref[g], 0.0).astype(BF16)
        bias_g = b_sp_t_ref[:, g:g + 1]
        for c in range(tm // CHUNK):
            s = _dot(w_g, v[c * CHUNK:(c + 1) * CHUNK, g * c_head:(g + 1) * c_head]) + bias_g
            c_buf[c * CHUNK:(c + 1) * CHUNK, g * c_head:(g + 1) * c_head] = s
    c_pre = (z_scr[:, 5 * d:6 * d] * c_buf[...]).astype(BF16)
    z_scr[:, 5 * d:7 * d] = _dot(xnb, w_in_ref[:, 5 * d:7 * d])
    yc = _dot(c_pre, w_c_out_ref[...])

    merged = (_sigmoid(z_scr[:, 7 * d:8 * d]) * ya + _sigmoid(z_scr[:, 8 * d:9 * d]) * yb
              + _sigmoid(z_scr[:, 9 * d:10 * d]) * yc).astype(BF16)
    z_scr[:, 7 * d:10 * d] = _dot(xnb, w_in_ref[:, 7 * d:10 * d])
    h = _layer_norm(alpha * x + _dot(merged, w_o_ref[...]), ln_g_ref[...], ln_b_ref[...])
    h_ref[...] = h
    _router_group_t(h, rw_t_ref[...], rb_ref[...], grp_ref)


def _mixer_sample_kernel(x_ref, ctx_a_ref, ctx_b_ref, w_in_ref, ca_w_ref, w_a_out_ref, cb_w_ref, cb_bias_ref,
                         nb_g_ref, nb_b_ref, w_b_out_ref, vn_g_ref, vn_b_ref, sp_scale_ref, sp_bias_ref,
                         w_c_out_ref, w_o_ref, ln_g_ref, ln_b_ref, rw_t_ref, rb_ref,
                         h_ref, gt_ref, a_in_ref, b_glu_ref, v_ref, *, alpha):
    d = x_ref.shape[-1]
    x = x_ref[...]
    xb = x.astype(BF16)

    za = _dot(xb, w_in_ref[:, 0:3 * d])
    a_in = za[:, d:2 * d] * za[:, 2 * d:3 * d]
    conv_a = ca_w_ref[K_A - 1:K_A, :] * a_in
    for k in range(K_A - 1):
        conv_a = conv_a + ca_w_ref[k:k + 1, :] * ctx_a_ref[k]
    a_in_ref[...] = a_in
    ya = _dot((za[:, 0:d] * conv_a).astype(BF16), w_a_out_ref[...])

    zb = _dot(xb, w_in_ref[:, 3 * d:5 * d])
    b_glu = zb[:, 0:d] * _sigmoid(zb[:, d:2 * d])
    conv_b = cb_w_ref[K_B - 1:K_B, :] * b_glu
    for k in range(K_B - 1):
        conv_b = conv_b + cb_w_ref[k:k + 1, :] * ctx_b_ref[k]
    b_glu_ref[...] = b_glu
    b_n = _layer_norm(conv_b + cb_bias_ref[...], nb_g_ref[...], nb_b_ref[...])
    yb = _dot((b_n * _sigmoid(b_n)).astype(BF16), w_b_out_ref[...])

    zc = _dot(xb, w_in_ref[:, 5 * d:7 * d])
    v = _layer_norm(zc[:, d:2 * d], vn_g_ref[...], vn_b_ref[...])
    v_ref[...] = v
    yc = _dot((zc[:, 0:d] * (v * sp_scale_ref[...] + sp_bias_ref[...])).astype(BF16), w_c_out_ref[...])

    zg = _dot(xb, w_in_ref[:, 7 * d:10 * d])
    h = _merge_and_norm(x, zg, ya, yb, yc, w_o_ref, ln_g_ref, ln_b_ref, alpha)
    h_ref[...] = h
    _router_gates_t(h, rw_t_ref[...], rb_ref[...], gt_ref)


def _moe_dense_kernel(h_ref, g_ref, wg_ref, wu_ref, wd_ref, ln_g_ref, ln_b_ref, out_ref, acc_ref, hb_ref, *,
                      alpha):
    e = pl.program_id(1)

    @pl.when(e == 0)
    def _():
        hb_ref[...] = h_ref[...].astype(BF16)
        acc_ref[...] = jnp.zeros_like(acc_ref)

    hb = hb_ref[...]
    gate = _dot(hb, wg_ref[...])
    u = gate * _sigmoid(gate) * _dot(hb, wu_ref[...])
    y = _dot(u.astype(BF16), wd_ref[...])
    g = g_ref[...]
    lane = lax.broadcasted_iota(jnp.int32, g.shape, 1)
    g_e = jnp.sum(jnp.where(lane == e, g, 0.0), axis=-1, keepdims=True)
    acc_ref[...] += g_e * y

    @pl.when(e == pl.num_programs(1) - 1)
    def _():
        out_ref[...] = _layer_norm(alpha * h_ref[...] + acc_ref[...], ln_g_ref[...], ln_b_ref[...])


def _resident(shape):
    zeros = (0,) * len(shape)
    return pl.BlockSpec(shape, lambda *_: zeros, pipeline_mode=pl.Buffered(1))


def _mixer_prompt(x, p, alpha):
    n_seq, seq, d = x.shape
    tm = MIX_TM
    seg = tm // SUBLANES
    assert seq % tm == 0 and tm % CHUNK == 0 and seg % SUBLANES == 0 and seg >= K_B - 1
    n_i = seq // tm
    n_tiles = n_seq * n_i
    weights = (p['w_in'], p['conv_a_w'], p['w_a_out'], p['conv_b_w8'], p['conv_b_bias'], p['norm_b_g'],
               p['norm_b_b'], p['w_b_out'], p['v_norm_g'], p['v_norm_b'], p['w_spatial'], p['b_spatial_t'],
               p['w_c_out'], p['w_o'], p['ln1_g'], p['ln1_b'], p['router_w_t'], p['router_b'])
    x2 = x.reshape(n_seq * seq, d)
    return pl.pallas_call(
        functools.partial(_mixer_prompt_kernel, alpha=alpha, tiles_per_seq=n_i),
        grid=(n_tiles,),
        in_specs=[pl.BlockSpec((tm, d), lambda n: (n, 0)),
                  pl.BlockSpec((tm, d), lambda n: (jnp.minimum(n + 1, n_tiles - 1), 0))]
        + [_resident(w.shape) for w in weights],
        out_specs=[pl.BlockSpec((tm, d), lambda n: (n, 0)),
                   pl.BlockSpec((1, tm), lambda n: (0, n)),
                   pl.BlockSpec((None, K_A - 1, d), lambda n: (n // n_i, 0, 0)),
                   pl.BlockSpec((None, K_B - 1, d), lambda n: (n // n_i, 0, 0))],
        out_shape=[jax.ShapeDtypeStruct((n_seq * seq, d), F32),
                   jax.ShapeDtypeStruct((1, n_seq * seq), jnp.int32),
                   jax.ShapeDtypeStruct((n_seq, K_A - 1, d), F32),
                   jax.ShapeDtypeStruct((n_seq, K_B - 1, d), F32)],
        scratch_shapes=[pltpu.VMEM((tm, p['w_in'].shape[1]), F32),
                        pltpu.VMEM((A_PAD + tm, d), F32), pltpu.VMEM((seg + tm, d), F32),
                        pltpu.VMEM(((K_B - 1) * SUBLANES + tm, d), F32),
                        pltpu.VMEM((tm, d), F32), pltpu.VMEM((tm, d), F32)],
        compiler_params=pltpu.CompilerParams(dimension_semantics=("arbitrary",),
                                             vmem_limit_bytes=VMEM_LIMIT),
        name="mixer_prompt",
    )(x2, x2, *weights)


def _mixer_sample(x, ctx_a_t, ctx_b_t, p, alpha):
    n, d = x.shape
    weights = (p['w_in'], p['conv_a_w'], p['w_a_out'], p['conv_b_w'], p['conv_b_bias'], p['norm_b_g'],
               p['norm_b_b'], p['w_b_out'], p['v_norm_g'], p['v_norm_b'], p['sp_scale'], p['sp_bias'],
               p['w_c_out'], p['w_o'], p['ln1_g'], p['ln1_b'], p['router_w_t'], p['router_b'])
    return pl.pallas_call(
        functools.partial(_mixer_sample_kernel, alpha=alpha),
        out_shape=[jax.ShapeDtypeStruct((n, d), F32),
                   jax.ShapeDtypeStruct((N_EXPERTS, n), F32),
                   jax.ShapeDtypeStruct((n, d), F32),
                   jax.ShapeDtypeStruct((n, d), F32),
                   jax.ShapeDtypeStruct((n, d), F32)],
        compiler_params=pltpu.CompilerParams(vmem_limit_bytes=VMEM_LIMIT),
        name="mixer_sample",
    )(x, ctx_a_t, ctx_b_t, *weights)


def _moe_dense(h, gates, p, alpha, tm):
    t, d = h.shape
    assert t % tm == 0
    d_e = p['expert_w_gate'].shape[-1]
    return pl.pallas_call(
        functools.partial(_moe_dense_kernel, alpha=alpha),
        grid=(t // tm, N_EXPERTS),
        in_specs=[pl.BlockSpec((tm, d), lambda i, e: (i, 0)),
                  pl.BlockSpec((tm, N_EXPERTS), lambda i, e: (i, 0)),
                  pl.BlockSpec((None, d, d_e), lambda i, e: (e, 0, 0)),
                  pl.BlockSpec((None, d, d_e), lambda i, e: (e, 0, 0)),
                  pl.BlockSpec((None, d_e, d), lambda i, e: (e, 0, 0)),
                  pl.BlockSpec((1, d), lambda i, e: (0, 0)),
                  pl.BlockSpec((1, d), lambda i, e: (0, 0))],
        out_specs=pl.BlockSpec((tm, d), lambda i, e: (i, 0)),
        out_shape=jax.ShapeDtypeStruct((t, d), F32),
        scratch_shapes=[pltpu.VMEM((tm, d), F32), pltpu.VMEM((tm, d), BF16)],
        compiler_params=pltpu.CompilerParams(dimension_semantics=("arbitrary", "arbitrary"),
                                             vmem_limit_bytes=VMEM_LIMIT),
        name="moe_dense",
    )(h, gates, p['expert_w_gate'], p['expert_w_up'], p['expert_w_down'], p['ln2_g'], p['ln2_b'])


def _moe_group_kernel(tile_group_ref, rb_ref, h_ref, rw_ref, wg_ref, wu_ref, wd_ref, ln_g_ref, ln_b_ref,
                      out_ref, acc_ref, hb_ref, gate_ref, *, alpha):
    i = pl.program_id(0)
    j = pl.program_id(1)
    grp = tile_group_ref[i]
    live = grp < N_EXPERT_GROUPS
    last = j == pl.num_programs(1) - 1

    @pl.when(jnp.logical_and(live, j == 0))
    def _():
        h = h_ref[...]
        hb_ref[...] = h.astype(BF16)
        acc_ref[...] = jnp.zeros_like(acc_ref)
        logits = [jnp.sum(h * rw_ref[k:k + 1, :], axis=-1, keepdims=True) + rb_ref[grp * EXPERTS_PER_GROUP + k]
                  for k in range(EXPERTS_PER_GROUP)]
        top = functools.reduce(jnp.maximum, logits)
        p = [jnp.exp(l - top) for l in logits]
        score, selected = _top2(p)
        inv = 1.0 / score
        lane = lax.broadcasted_iota(jnp.int32, gate_ref.shape, 1)
        gates = jnp.zeros(gate_ref.shape, F32)
        for k in range(EXPERTS_PER_GROUP):
            gates = jnp.where(lane == k, jnp.where(selected[k], p[k] * inv, 0.0), gates)
        gate_ref[...] = gates

    @pl.when(live)
    def _():
        hb = hb_ref[...]
        gate = _dot(hb, wg_ref[...])
        u = gate * _sigmoid(gate) * _dot(hb, wu_ref[...])
        y = _dot(u.astype(BF16), wd_ref[...])
        lane = lax.broadcasted_iota(jnp.int32, gate_ref.shape, 1)
        g_j = jnp.sum(jnp.where(lane == j, gate_ref[...], 0.0), axis=-1, keepdims=True)
        acc_ref[...] += g_j * y

    @pl.when(jnp.logical_and(live, last))
    def _():
        out_ref[...] = _layer_norm(alpha * h_ref[...] + acc_ref[...], ln_g_ref[...], ln_b_ref[...])

    @pl.when(jnp.logical_and(jnp.logical_not(live), last))
    def _():
        out_ref[...] = jnp.zeros_like(out_ref)


def _row_permute_kernel(idx_ref, src_ref, *refs, scatter, rows):
    dst_ref, sem = refs[-2], refs[-1]
    base = pl.program_id(0) * rows

    def row_copy(t):
        r = idx_ref[0, 0, t]
        if scatter:
            return pltpu.make_async_copy(src_ref.at[pl.ds(base + t, 1)], dst_ref.at[pl.ds(r, 1)], sem)
        return pltpu.make_async_copy(src_ref.at[pl.ds(r, 1)], dst_ref.at[pl.ds(base + t, 1)], sem)

    def issue(t, carry):
        row_copy(t).start()
        return carry

    lax.fori_loop(0, rows, issue, 0, unroll=8)
    pltpu.make_async_copy(src_ref.at[pl.ds(0, rows)], dst_ref.at[pl.ds(0, rows)], sem).wait()


def _plan_groups(grp, tm):
    t = grp.shape[0]
    onehot = (grp[None, :] == jnp.arange(N_EXPERT_GROUPS, dtype=jnp.int32)[:, None]).astype(jnp.int32)
    incl = jnp.cumsum(onehot, axis=1)
    counts = incl[:, -1]
    rank = jnp.sum(onehot * (incl - 1), axis=0)
    padded = (counts + tm - 1) // tm * tm
    ends = jnp.cumsum(padded)
    pos = jnp.sum(onehot * (ends - padded)[:, None], axis=0) + rank
    n_tiles = t // tm + N_EXPERT_GROUPS
    starts = jnp.arange(n_tiles, dtype=jnp.int32) * tm
    tile_group = jnp.sum((starts[:, None] >= ends[None, :]).astype(jnp.int32), axis=1)
    return pos.astype(jnp.int32), tile_group.astype(jnp.int32)


def _permute_rows(src, idx, n_out, scatter):
    n_idx = idx.shape[0]
    d = src.shape[1]
    rows = PERMUTE_ROWS
    assert n_idx % rows == 0
    any_spec = pl.BlockSpec(memory_space=pl.ANY)
    operands = [idx.reshape(n_idx // rows, 1, rows), src]
    in_specs = [pl.BlockSpec((1, 1, rows), lambda i: (i, 0, 0), memory_space=pltpu.SMEM), any_spec]
    aliases = {}
    if scatter:
        operands.append(jnp.zeros((n_out, d), src.dtype))
        in_specs.append(any_spec)
        aliases = {2: 0}
    return pl.pallas_call(
        functools.partial(_row_permute_kernel, scatter=scatter, rows=rows),
        grid=(n_idx // rows,),
        in_specs=in_specs,
        out_specs=any_spec,
        out_shape=jax.ShapeDtypeStruct((n_out, d), src.dtype),
        scratch_shapes=[pltpu.SemaphoreType.DMA],
        input_output_aliases=aliases,
        compiler_params=pltpu.CompilerParams(dimension_semantics=("arbitrary",)),
        name="scatter_rows" if scatter else "gather_rows",
    )(*operands)


def _moe_grouped(h, tile_group, p, alpha, tm):
    r, d = h.shape
    d_e = p['expert_w_gate'].shape[-1]
    n_tiles = r // tm
    assert tile_group.shape == (n_tiles,)

    def expert(i, j, tg, rb):
        return (jnp.minimum(tg[i], N_EXPERT_GROUPS - 1) * EXPERTS_PER_GROUP + j, 0, 0)

    grid_spec = pltpu.PrefetchScalarGridSpec(
        num_scalar_prefetch=2,
        grid=(n_tiles, EXPERTS_PER_GROUP),
        in_specs=[pl.BlockSpec((tm, d), lambda i, j, tg, rb: (i, 0)),
                  pl.BlockSpec((None, EXPERTS_PER_GROUP, d),
                               lambda i, j, tg, rb: (jnp.minimum(tg[i], N_EXPERT_GROUPS - 1), 0, 0)),
                  pl.BlockSpec((None, d, d_e), expert),
                  pl.BlockSpec((None, d, d_e), expert),
                  pl.BlockSpec((None, d_e, d), expert),
                  pl.BlockSpec((1, d), lambda i, j, tg, rb: (0, 0)),
                  pl.BlockSpec((1, d), lambda i, j, tg, rb: (0, 0))],
        out_specs=pl.BlockSpec((tm, d), lambda i, j, tg, rb: (i, 0)),
        scratch_shapes=[pltpu.VMEM((tm, d), F32), pltpu.VMEM((tm, d), BF16), pltpu.VMEM((tm, LANES), F32)])
    return pl.pallas_call(
        functools.partial(_moe_group_kernel, alpha=alpha),
        grid_spec=grid_spec,
        out_shape=jax.ShapeDtypeStruct((r, d), F32),
        compiler_params=pltpu.CompilerParams(dimension_semantics=("arbitrary", "arbitrary"),
                                             vmem_limit_bytes=VMEM_LIMIT),
        name="moe_grouped",
    )(tile_group, p['router_b_flat'], h, p['router_w_groups'], p['expert_w_gate'], p['expert_w_up'],
      p['expert_w_down'], p['ln2_g'], p['ln2_b'])


def kernel(x_prompt, x_sample, state_conv_a, state_conv_b, w_in, conv_a_w, w_a_out, conv_b_w, conv_b_bias, norm_b_g, norm_b_b, w_b_out, v_norm_g, v_norm_b, w_spatial, b_spatial, w_c_out, w_o, ln1_g, ln1_b, router_w, router_b, expert_w_gate, expert_w_up, expert_w_down, ln2_g, ln2_b):
    depth = w_in.shape[0]
    n_seq, seq, d = x_prompt.shape
    n_dec, dec_seq, _ = x_sample.shape
    assert dec_seq == 1
    alpha = (2 * depth) ** 0.25
    c_head = d // C_GROUPS

    xp = x_prompt
    xs = x_sample.reshape(n_dec, d)
    router_w_t = router_w.T
    router_b_col = router_b.reshape(N_EXPERTS, 1)
    pa, pb, sa, sb, sv = [], [], [], [], []
    for l in range(depth):
        row = lambda a: a[l].reshape(1, d)
        p = dict(w_in=w_in[l].astype(BF16), conv_a_w=conv_a_w[l], w_a_out=w_a_out[l].astype(BF16),
                 conv_b_w=conv_b_w[l], conv_b_bias=row(conv_b_bias), norm_b_g=row(norm_b_g),
                 conv_b_w8=jnp.broadcast_to(conv_b_w[l][:, None, :], (K_B, SUBLANES, d)),
                 norm_b_b=row(norm_b_b), w_b_out=w_b_out[l].astype(BF16), v_norm_g=row(v_norm_g),
                 v_norm_b=row(v_norm_b), w_spatial=w_spatial[l], b_spatial_t=b_spatial[l].T,
                 sp_scale=jnp.repeat(w_spatial[l, :, 0, 0], c_head).reshape(1, d),
                 sp_bias=jnp.repeat(b_spatial[l, :, 0], c_head).reshape(1, d),
                 w_c_out=w_c_out[l].astype(BF16), w_o=w_o[l].astype(BF16), ln1_g=row(ln1_g), ln1_b=row(ln1_b),
                 router_w_t=router_w_t, router_b=router_b_col,
                 router_w_groups=router_w_t.reshape(N_EXPERT_GROUPS, EXPERTS_PER_GROUP, d),
                 router_b_flat=router_b,
                 expert_w_gate=expert_w_gate[l].astype(BF16), expert_w_up=expert_w_up[l].astype(BF16),
                 expert_w_down=expert_w_down[l].astype(BF16), ln2_g=row(ln2_g), ln2_b=row(ln2_b))

        hp, grp, na, nb = _mixer_prompt(xp, p, alpha)
        pa.append(na)
        pb.append(nb)
        pos, tile_group = _plan_groups(grp.reshape(n_seq * seq), MOE_TM)
        hp_sorted = _permute_rows(hp, pos, tile_group.shape[0] * MOE_TM, scatter=True)
        yp_sorted = _moe_grouped(hp_sorted, tile_group, p, alpha, MOE_TM)
        xp = _permute_rows(yp_sorted, pos, n_seq * seq, scatter=False).reshape(n_seq, seq, d)

        hs, gts, a_in, b_glu, v = _mixer_sample(xs, jnp.swapaxes(state_conv_a[l], 0, 1),
                                                jnp.swapaxes(state_conv_b[l], 0, 1), p, alpha)
        sa.append(jnp.concatenate([state_conv_a[l][:, 1:], a_in[:, None, :]], axis=1))
        sb.append(jnp.concatenate([state_conv_b[l][:, 1:], b_glu[:, None, :]], axis=1))
        sv.append(v.reshape(n_dec, 1, d))
        xs = _moe_dense(hs, gts.T, p, alpha, n_dec)

    return (xp, xs.reshape(n_dec, 1, d), jnp.stack(pa), jnp.stack(pb), jnp.stack(sa), jnp.stack(sb),
            jnp.stack(sv))
```

```python
import functools

import jax
import jax.numpy as jnp
from jax import lax
from jax.experimental import pallas as pl
from jax.experimental.pallas import tpu as pltpu

F32 = jnp.float32
BF16 = jnp.bfloat16

K_A = 3
K_B = 31
CHUNK = 128
C_GROUPS = 8
N_EXPERTS = 16
N_EXPERT_GROUPS = 4
EXPERTS_PER_GROUP = N_EXPERTS // N_EXPERT_GROUPS
LN_EPS = 1e-5

SUBLANES = 8
LANES = 128
A_PAD = SUBLANES
CONV_GROUPS = 16
MIX_TM = 256
MOE_TM = 1024
VMEM_LIMIT = 56 * 1024 * 1024


def _sigmoid(x):
    return 1.0 / (1.0 + jnp.exp(-x))


def _layer_norm(x, g, b):
    mu = jnp.mean(x, axis=-1, keepdims=True)
    xc = x - mu
    var = jnp.mean(xc * xc, axis=-1, keepdims=True)
    return xc * lax.rsqrt(var + LN_EPS) * g + b


def _dot(a, b):
    return jnp.dot(a, b, preferred_element_type=F32)


def _interleave_segments(a, seg):
    n, d = a.shape
    return jnp.swapaxes(a.reshape(n // seg, seg, d), 0, 1).reshape(n, d)


def _first_max(vals):
    m = functools.reduce(jnp.maximum, vals)
    hits, taken = [], None
    for v in vals:
        hit = v == m
        if taken is None:
            taken = hit
        else:
            hit = jnp.logical_and(hit, jnp.logical_not(taken))
            taken = jnp.logical_or(taken, hit)
        hits.append(hit)
    return m, hits


def _router_gates_t(h, rw_t, rb, gt_ref):
    logits = lax.dot_general(rw_t, h, (((1,), (1,)), ((), ())), precision=lax.Precision.HIGHEST,
                             preferred_element_type=F32) + rb
    p = jnp.exp(logits - jnp.max(logits, axis=0, keepdims=True))
    rows = [p[e:e + 1, :] for e in range(N_EXPERTS)]
    scores, selected = [], []
    for g in range(N_EXPERT_GROUPS):
        a = rows[g * EXPERTS_PER_GROUP:(g + 1) * EXPERTS_PER_GROUP]
        m1, is1 = _first_max(a)
        rest = [jnp.where(hit, -1.0, v) for hit, v in zip(is1, a)]
        m2, is2 = _first_max(rest)
        scores.append(m1 + m2)
        selected.append([jnp.logical_or(x, y) for x, y in zip(is1, is2)])
    _, best = _first_max(scores)
    for g in range(N_EXPERT_GROUPS):
        inv = 1.0 / scores[g]
        for j in range(EXPERTS_PER_GROUP):
            e = g * EXPERTS_PER_GROUP + j
            keep = jnp.logical_and(best[g], selected[g][j])
            gt_ref[e:e + 1, :] = jnp.where(keep, rows[e] * inv, 0.0)


def _merge_and_norm(x, zg, ya, yb, yc, w_o_ref, ln_g_ref, ln_b_ref, alpha):
    d = x.shape[-1]
    merged = (_sigmoid(zg[:, :d]) * ya + _sigmoid(zg[:, d:2 * d]) * yb + _sigmoid(zg[:, 2 * d:]) * yc)
    m = _dot(merged.astype(BF16), w_o_ref[...])
    return _layer_norm(alpha * x + m, ln_g_ref[...], ln_b_ref[...])


def _mixer_prompt_kernel(x_ref, xn_ref, w_in_ref, ca_w_ref, w_a_out_ref, cb_w_ref, cb_bias_ref, nb_g_ref,
                         nb_b_ref, w_b_out_ref, vn_g_ref, vn_b_ref, w_sp_ref, b_sp_t_ref, w_c_out_ref, w_o_ref,
                         ln_g_ref, ln_b_ref, rw_t_ref, rb_ref,
                         h_ref, gt_ref, new_a_ref, new_b_ref,
                         z_scr, a_ext, b_ext, b_perm, cb_buf, c_buf, *, alpha, tiles_per_seq):
    tm, d = x_ref.shape
    n = pl.program_id(0)

    @pl.when(n == 0)
    def _():
        xb0 = x_ref[...].astype(BF16)
        for c0, c1 in ((0, 3 * d), (3 * d, 5 * d), (5 * d, 7 * d), (7 * d, 10 * d)):
            z_scr[:, c0:c1] = _dot(xb0, w_in_ref[:, c0:c1])

    @pl.when(n % tiles_per_seq == 0)
    def _():
        a_ext[0:A_PAD, :] = jnp.zeros((A_PAD, d), F32)
        b_ext[0:tm // SUBLANES, :] = jnp.zeros((tm // SUBLANES, d), F32)

    x = x_ref[...]
    xnb = xn_ref[...].astype(BF16)

    a_in = z_scr[:, d:2 * d] * z_scr[:, 2 * d:3 * d]
    a_ext[A_PAD:A_PAD + tm, :] = a_in
    conv_a = (ca_w_ref[0:1, :] * a_ext[A_PAD - 2:A_PAD - 2 + tm, :]
              + ca_w_ref[1:2, :] * a_ext[A_PAD - 1:A_PAD - 1 + tm, :]
              + ca_w_ref[2:3, :] * a_in)
    new_a_ref[...] = a_ext[A_PAD + tm - (K_A - 1):A_PAD + tm, :]
    a_ext[0:A_PAD, :] = a_ext[tm:tm + A_PAD, :]
    a_pre = (z_scr[:, 0:d] * conv_a).astype(BF16)
    z_scr[:, 0:3 * d] = _dot(xnb, w_in_ref[:, 0:3 * d])
    ya = _dot(a_pre, w_a_out_ref[...])

    seg = tm // SUBLANES
    halo = (K_B - 1) * SUBLANES
    b_ext[seg:seg + tm, :] = z_scr[:, 3 * d:4 * d] * _sigmoid(z_scr[:, 4 * d:5 * d])
    z_scr[:, 3 * d:5 * d] = _dot(xnb, w_in_ref[:, 3 * d:5 * d])
    b_perm[0:halo, :] = _interleave_segments(b_ext[0:tm, :], seg)[tm - halo:, :]
    b_perm[halo:halo + tm, :] = _interleave_segments(b_ext[seg:seg + tm, :], seg)
    for c0 in range(0, d, LANES):
        for r0 in range(0, tm, CONV_GROUPS * SUBLANES):
            accs = [None] * CONV_GROUPS
            for k in range(K_B):
                w_k = cb_w_ref[k, :, c0:c0 + LANES]
                for u in range(CONV_GROUPS):
                    r = SUBLANES * (k + u) + r0
                    tap = w_k * b_perm[r:r + SUBLANES, c0:c0 + LANES]
                    accs[u] = tap if k == 0 else accs[u] + tap
            for u in range(CONV_GROUPS):
                cb_buf[r0 + SUBLANES * u:r0 + SUBLANES * (u + 1), c0:c0 + LANES] = accs[u]
    new_b_ref[...] = b_ext[seg + tm - (K_B - 1):seg + tm, :]
    b_ext[0:seg, :] = b_ext[tm:tm + seg, :]
    conv_b = _interleave_segments(cb_buf[...], SUBLANES)
    b_n = _layer_norm(conv_b + cb_bias_ref[...], nb_g_ref[...], nb_b_ref[...])
    yb = _dot((b_n * _sigmoid(b_n)).astype(BF16), w_b_out_ref[...])

    v = _layer_norm(z_scr[:, 6 * d:7 * d], vn_g_ref[...], vn_b_ref[...]).astype(BF16)
    row = lax.broadcasted_iota(jnp.int32, (CHUNK, CHUNK), 0)
    col = lax.broadcasted_iota(jnp.int32, (CHUNK, CHUNK), 1)
    c_head = d // C_GROUPS
    for g in range(C_GROUPS):
        w_g = jnp.where(col <= row, w_sp_ref[g], 0.0).astype(BF16)
        bias_g = b_sp_t_ref[:, g:g + 1]
        for c in range(tm // CHUNK):
            s = _dot(w_g, v[c * CHUNK:(c + 1) * CHUNK, g * c_head:(g + 1) * c_head]) + bias_g
            c_buf[c * CHUNK:(c + 1) * CHUNK, g * c_head:(g + 1) * c_head] = s
    c_pre = (z_scr[:, 5 * d:6 * d] * c_buf[...]).astype(BF16)
    z_scr[:, 5 * d:7 * d] = _dot(xnb, w_in_ref[:, 5 * d:7 * d])
    yc = _dot(c_pre, w_c_out_ref[...])

    merged = (_sigmoid(z_scr[:, 7 * d:8 * d]) * ya + _sigmoid(z_scr[:, 8 * d:9 * d]) * yb
              + _sigmoid(z_scr[:, 9 * d:10 * d]) * yc).astype(BF16)
    z_scr[:, 7 * d:10 * d] = _dot(xnb, w_in_ref[:, 7 * d:10 * d])
    h = _layer_norm(alpha * x + _dot(merged, w_o_ref[...]), ln_g_ref[...], ln_b_ref[...])
    h_ref[...] = h
    _router_gates_t(h, rw_t_ref[...], rb_ref[...], gt_ref)


def _mixer_sample_kernel(x_ref, ctx_a_ref, ctx_b_ref, w_in_ref, ca_w_ref, w_a_out_ref, cb_w_ref, cb_bias_ref,
                         nb_g_ref, nb_b_ref, w_b_out_ref, vn_g_ref, vn_b_ref, sp_scale_ref, sp_bias_ref,
                         w_c_out_ref, w_o_ref, ln_g_ref, ln_b_ref, rw_t_ref, rb_ref,
                         h_ref, gt_ref, a_in_ref, b_glu_ref, v_ref, *, alpha):
    d = x_ref.shape[-1]
    x = x_ref[...]
    xb = x.astype(BF16)

    za = _dot(xb, w_in_ref[:, 0:3 * d])
    a_in = za[:, d:2 * d] * za[:, 2 * d:3 * d]
    conv_a = ca_w_ref[K_A - 1:K_A, :] * a_in
    for k in range(K_A - 1):
        conv_a = conv_a + ca_w_ref[k:k + 1, :] * ctx_a_ref[k]
    a_in_ref[...] = a_in
    ya = _dot((za[:, 0:d] * conv_a).astype(BF16), w_a_out_ref[...])

    zb = _dot(xb, w_in_ref[:, 3 * d:5 * d])
    b_glu = zb[:, 0:d] * _sigmoid(zb[:, d:2 * d])
    conv_b = cb_w_ref[K_B - 1:K_B, :] * b_glu
    for k in range(K_B - 1):
        conv_b = conv_b + cb_w_ref[k:k + 1, :] * ctx_b_ref[k]
    b_glu_ref[...] = b_glu
    b_n = _layer_norm(conv_b + cb_bias_ref[...], nb_g_ref[...], nb_b_ref[...])
    yb = _dot((b_n * _sigmoid(b_n)).astype(BF16), w_b_out_ref[...])

    zc = _dot(xb, w_in_ref[:, 5 * d:7 * d])
    v = _layer_norm(zc[:, d:2 * d], vn_g_ref[...], vn_b_ref[...])
    v_ref[...] = v
    yc = _dot((zc[:, 0:d] * (v * sp_scale_ref[...] + sp_bias_ref[...])).astype(BF16), w_c_out_ref[...])

    zg = _dot(xb, w_in_ref[:, 7 * d:10 * d])
    h = _merge_and_norm(x, zg, ya, yb, yc, w_o_ref, ln_g_ref, ln_b_ref, alpha)
    h_ref[...] = h
    _router_gates_t(h, rw_t_ref[...], rb_ref[...], gt_ref)


def _moe_dense_kernel(h_ref, g_ref, wg_ref, wu_ref, wd_ref, ln_g_ref, ln_b_ref, out_ref, acc_ref, hb_ref, *,
                      alpha):
    e = pl.program_id(1)

    @pl.when(e == 0)
    def _():
        hb_ref[...] = h_ref[...].astype(BF16)
        acc_ref[...] = jnp.zeros_like(acc_ref)

    hb = hb_ref[...]
    gate = _dot(hb, wg_ref[...].astype(BF16))
    u = gate * _sigmoid(gate) * _dot(hb, wu_ref[...].astype(BF16))
    y = _dot(u.astype(BF16), wd_ref[...].astype(BF16))
    g = g_ref[...]
    lane = lax.broadcasted_iota(jnp.int32, g.shape, 1)
    g_e = jnp.sum(jnp.where(lane == e, g, 0.0), axis=-1, keepdims=True)
    acc_ref[...] += g_e * y

    @pl.when(e == pl.num_programs(1) - 1)
    def _():
        out_ref[...] = _layer_norm(alpha * h_ref[...] + acc_ref[...], ln_g_ref[...], ln_b_ref[...])


def _resident(shape):
    zeros = (0,) * len(shape)
    return pl.BlockSpec(shape, lambda *_: zeros, pipeline_mode=pl.Buffered(1))


def _mixer_prompt(x, p, alpha):
    n_seq, seq, d = x.shape
    tm = MIX_TM
    seg = tm // SUBLANES
    assert seq % tm == 0 and tm % CHUNK == 0 and seg % SUBLANES == 0 and seg >= K_B - 1
    n_i = seq // tm
    n_tiles = n_seq * n_i
    weights = (p['w_in'], p['conv_a_w'], p['w_a_out'], p['conv_b_w8'], p['conv_b_bias'], p['norm_b_g'],
               p['norm_b_b'], p['w_b_out'], p['v_norm_g'], p['v_norm_b'], p['w_spatial'], p['b_spatial_t'],
               p['w_c_out'], p['w_o'], p['ln1_g'], p['ln1_b'], p['router_w_t'], p['router_b'])
    x2 = x.reshape(n_seq * seq, d)
    h, gt, new_a, new_b = pl.pallas_call(
        functools.partial(_mixer_prompt_kernel, alpha=alpha, tiles_per_seq=n_i),
        grid=(n_tiles,),
        in_specs=[pl.BlockSpec((tm, d), lambda n: (n, 0)),
                  pl.BlockSpec((tm, d), lambda n: (jnp.minimum(n + 1, n_tiles - 1), 0))]
        + [_resident(w.shape) for w in weights],
        out_specs=[pl.BlockSpec((tm, d), lambda n: (n, 0)),
                   pl.BlockSpec((N_EXPERTS, tm), lambda n: (0, n)),
                   pl.BlockSpec((None, K_A - 1, d), lambda n: (n // n_i, 0, 0)),
                   pl.BlockSpec((None, K_B - 1, d), lambda n: (n // n_i, 0, 0))],
        out_shape=[jax.ShapeDtypeStruct((n_seq * seq, d), F32),
                   jax.ShapeDtypeStruct((N_EXPERTS, n_seq * seq), F32),
                   jax.ShapeDtypeStruct((n_seq, K_A - 1, d), F32),
                   jax.ShapeDtypeStruct((n_seq, K_B - 1, d), F32)],
        scratch_shapes=[pltpu.VMEM((tm, p['w_in'].shape[1]), F32),
                        pltpu.VMEM((A_PAD + tm, d), F32), pltpu.VMEM((seg + tm, d), F32),
                        pltpu.VMEM(((K_B - 1) * SUBLANES + tm, d), F32),
                        pltpu.VMEM((tm, d), F32), pltpu.VMEM((tm, d), F32)],
        compiler_params=pltpu.CompilerParams(dimension_semantics=("arbitrary",),
                                             vmem_limit_bytes=VMEM_LIMIT),
        name="mixer_prompt",
    )(x2, x2, *weights)
    return h, gt, new_a, new_b


def _mixer_sample(x, ctx_a_t, ctx_b_t, p, alpha):
    n, d = x.shape
    weights = (p['w_in'], p['conv_a_w'], p['w_a_out'], p['conv_b_w'], p['conv_b_bias'], p['norm_b_g'],
               p['norm_b_b'], p['w_b_out'], p['v_norm_g'], p['v_norm_b'], p['sp_scale'], p['sp_bias'],
               p['w_c_out'], p['w_o'], p['ln1_g'], p['ln1_b'], p['router_w_t'], p['router_b'])
    return pl.pallas_call(
        functools.partial(_mixer_sample_kernel, alpha=alpha),
        out_shape=[jax.ShapeDtypeStruct((n, d), F32),
                   jax.ShapeDtypeStruct((N_EXPERTS, n), F32),
                   jax.ShapeDtypeStruct((n, d), F32),
                   jax.ShapeDtypeStruct((n, d), F32),
                   jax.ShapeDtypeStruct((n, d), F32)],
        compiler_params=pltpu.CompilerParams(vmem_limit_bytes=VMEM_LIMIT),
        name="mixer_sample",
    )(x, ctx_a_t, ctx_b_t, *weights)


def _moe_dense(h, gates, p, alpha, tm, layer):
    t, d = h.shape
    assert t % tm == 0
    d_e = p['expert_w_gate'].shape[-1]
    return pl.pallas_call(
        functools.partial(_moe_dense_kernel, alpha=alpha),
        grid=(t // tm, N_EXPERTS),
        in_specs=[pl.BlockSpec((tm, d), lambda i, e: (i, 0)),
                  pl.BlockSpec((tm, N_EXPERTS), lambda i, e: (i, 0)),
                  pl.BlockSpec((None, None, d, d_e), lambda i, e: (layer, e, 0, 0)),
                  pl.BlockSpec((None, None, d, d_e), lambda i, e: (layer, e, 0, 0)),
                  pl.BlockSpec((None, None, d_e, d), lambda i, e: (layer, e, 0, 0)),
                  pl.BlockSpec((1, d), lambda i, e: (0, 0)),
                  pl.BlockSpec((1, d), lambda i, e: (0, 0))],
        out_specs=pl.BlockSpec((tm, d), lambda i, e: (i, 0)),
        out_shape=jax.ShapeDtypeStruct((t, d), F32),
        scratch_shapes=[pltpu.VMEM((tm, d), F32), pltpu.VMEM((tm, d), BF16)],
        compiler_params=pltpu.CompilerParams(dimension_semantics=("arbitrary", "arbitrary"),
                                             vmem_limit_bytes=VMEM_LIMIT),
        name="moe_dense",
    )(h, gates, p['expert_w_gate'], p['expert_w_up'], p['expert_w_down'], p['ln2_g'], p['ln2_b'])


def kernel(x_prompt, x_sample, state_conv_a, state_conv_b, w_in, conv_a_w, w_a_out, conv_b_w, conv_b_bias, norm_b_g, norm_b_b, w_b_out, v_norm_g, v_norm_b, w_spatial, b_spatial, w_c_out, w_o, ln1_g, ln1_b, router_w, router_b, expert_w_gate, expert_w_up, expert_w_down, ln2_g, ln2_b):
    depth = w_in.shape[0]
    n_seq, seq, d = x_prompt.shape
    n_dec, dec_seq, _ = x_sample.shape
    assert dec_seq == 1
    alpha = (2 * depth) ** 0.25
    c_head = d // C_GROUPS

    xp = x_prompt
    xs = x_sample.reshape(n_dec, d)
    router_w_t = router_w.T
    router_b_col = router_b.reshape(N_EXPERTS, 1)
    pa, pb, sa, sb, sv = [], [], [], [], []
    for l in range(depth):
        row = lambda a: a[l].reshape(1, d)
        p = dict(w_in=w_in[l].astype(BF16), conv_a_w=conv_a_w[l], w_a_out=w_a_out[l].astype(BF16),
                 conv_b_w=conv_b_w[l], conv_b_bias=row(conv_b_bias), norm_b_g=row(norm_b_g),
                 conv_b_w8=jnp.broadcast_to(conv_b_w[l][:, None, :], (K_B, SUBLANES, d)),
                 norm_b_b=row(norm_b_b), w_b_out=w_b_out[l].astype(BF16), v_norm_g=row(v_norm_g),
                 v_norm_b=row(v_norm_b), w_spatial=w_spatial[l], b_spatial_t=b_spatial[l].T,
                 sp_scale=jnp.repeat(w_spatial[l, :, 0, 0], c_head).reshape(1, d),
                 sp_bias=jnp.repeat(b_spatial[l, :, 0], c_head).reshape(1, d),
                 w_c_out=w_c_out[l].astype(BF16), w_o=w_o[l].astype(BF16), ln1_g=row(ln1_g), ln1_b=row(ln1_b),
                 router_w_t=router_w_t, router_b=router_b_col,
                 expert_w_gate=expert_w_gate, expert_w_up=expert_w_up, expert_w_down=expert_w_down,
                 ln2_g=row(ln2_g), ln2_b=row(ln2_b))

        hp, gtp, na, nb = _mixer_prompt(xp, p, alpha)
        pa.append(na)
        pb.append(nb)
        xp = _moe_dense(hp, gtp.T, p, alpha, MOE_TM, l).reshape(n_seq, seq, d)

        hs, gts, a_in, b_glu, v = _mixer_sample(xs, jnp.swapaxes(state_conv_a[l], 0, 1),
                                                jnp.swapaxes(state_conv_b[l], 0, 1), p, alpha)
        sa.append(jnp.concatenate([state_conv_a[l][:, 1:], a_in[:, None, :]], axis=1))
        sb.append(jnp.concatenate([state_conv_b[l][:, 1:], b_glu[:, None, :]], axis=1))
        sv.append(v.reshape(n_dec, 1, d))
        xs = _moe_dense(hs, gts.T, p, alpha, n_dec, l)

    return (xp, xs.reshape(n_dec, 1, d), jnp.stack(pa), jnp.stack(pb), jnp.stack(sa), jnp.stack(sb),
            jnp.stack(sv))
```

```python
import functools

import jax
import jax.numpy as jnp
from jax import lax
from jax.experimental import pallas as pl
from jax.experimental.pallas import tpu as pltpu

F32 = jnp.float32
BF16 = jnp.bfloat16

K_A = 3
K_B = 31
CHUNK = 128
C_GROUPS = 8
N_EXPERTS = 16
N_EXPERT_GROUPS = 4
EXPERTS_PER_GROUP = N_EXPERTS // N_EXPERT_GROUPS
LN_EPS = 1e-5

SUBLANES = 8
LANES = 128
A_PAD = SUBLANES
CONV_GROUPS = 16
B_TILE = 256
MIX_TM = 512
MOE_TM = 1024
VMEM_LIMIT = 58 * 1024 * 1024


def _sigmoid(x):
    return 1.0 / (1.0 + jnp.exp(-x))


def _layer_norm(x, g, b):
    mu = jnp.mean(x, axis=-1, keepdims=True)
    xc = x - mu
    var = jnp.mean(xc * xc, axis=-1, keepdims=True)
    return xc * lax.rsqrt(var + LN_EPS) * g + b


def _dot(a, b):
    return jnp.dot(a, b, preferred_element_type=F32)


def _interleave_segments(a, seg):
    n, d = a.shape
    return jnp.swapaxes(a.reshape(n // seg, seg, d), 0, 1).reshape(n, d)


def _first_max(vals):
    m = functools.reduce(jnp.maximum, vals)
    hits, taken = [], None
    for v in vals:
        hit = v == m
        if taken is None:
            taken = hit
        else:
            hit = jnp.logical_and(hit, jnp.logical_not(taken))
            taken = jnp.logical_or(taken, hit)
        hits.append(hit)
    return m, hits


def _router_gates_t(h, rw_t, rb, gt_ref):
    logits = lax.dot_general(rw_t, h, (((1,), (1,)), ((), ())), precision=lax.Precision.HIGHEST,
                             preferred_element_type=F32) + rb
    p = jnp.exp(logits - jnp.max(logits, axis=0, keepdims=True))
    rows = [p[e:e + 1, :] for e in range(N_EXPERTS)]
    scores, selected = [], []
    for g in range(N_EXPERT_GROUPS):
        a = rows[g * EXPERTS_PER_GROUP:(g + 1) * EXPERTS_PER_GROUP]
        m1, is1 = _first_max(a)
        rest = [jnp.where(hit, -1.0, v) for hit, v in zip(is1, a)]
        m2, is2 = _first_max(rest)
        scores.append(m1 + m2)
        selected.append([jnp.logical_or(x, y) for x, y in zip(is1, is2)])
    _, best = _first_max(scores)
    for g in range(N_EXPERT_GROUPS):
        inv = 1.0 / scores[g]
        for j in range(EXPERTS_PER_GROUP):
            e = g * EXPERTS_PER_GROUP + j
            keep = jnp.logical_and(best[g], selected[g][j])
            gt_ref[e:e + 1, :] = jnp.where(keep, rows[e] * inv, 0.0)


def _merge_and_norm(x, zg, ya, yb, yc, w_o_ref, ln_g_ref, ln_b_ref, alpha):
    d = x.shape[-1]
    merged = (_sigmoid(zg[:, :d]) * ya + _sigmoid(zg[:, d:2 * d]) * yb + _sigmoid(zg[:, 2 * d:]) * yc)
    m = _dot(merged.astype(BF16), w_o_ref[...])
    return _layer_norm(alpha * x + m, ln_g_ref[...], ln_b_ref[...])


def _mixer_prompt_kernel(x_ref, w_in_ref, ca_w_ref, w_a_out_ref, cb_w_ref, cb_bias_ref, nb_g_ref,
                         nb_b_ref, w_b_out_ref, vn_g_ref, vn_b_ref, w_sp_ref, b_sp_t_ref, w_c_out_ref, w_o_ref,
                         ln_g_ref, ln_b_ref, rw_t_ref, rb_ref,
                         h_ref, gt_ref, new_a_ref, new_b_ref,
                         a_ext, b_ext, b_perm, cb_buf, c_buf, *, alpha, tiles_per_seq):
    tm, d = x_ref.shape
    n = pl.program_id(0)

    @pl.when(n % tiles_per_seq == 0)
    def _():
        a_ext[0:A_PAD, :] = jnp.zeros((A_PAD, d), F32)
        b_ext[0:tm // SUBLANES, :] = jnp.zeros((tm // SUBLANES, d), F32)

    x = x_ref[...]
    xb = x.astype(BF16)

    seg = tm // SUBLANES
    halo = (K_B - 1) * SUBLANES
    for t0 in range(0, d, B_TILE):
        zv = _dot(xb, w_in_ref[:, 3 * d + t0:3 * d + t0 + B_TILE])
        zg = _dot(xb, w_in_ref[:, 4 * d + t0:4 * d + t0 + B_TILE])
        b_ext[seg:seg + tm, t0:t0 + B_TILE] = zv * _sigmoid(zg)
        b_perm[0:halo, t0:t0 + B_TILE] = _interleave_segments(b_ext[0:tm, t0:t0 + B_TILE], seg)[tm - halo:, :]
        b_perm[halo:halo + tm, t0:t0 + B_TILE] = _interleave_segments(b_ext[seg:seg + tm, t0:t0 + B_TILE], seg)
        for c0 in range(t0, t0 + B_TILE, LANES):
            for r0 in range(0, tm, CONV_GROUPS * SUBLANES):
                accs = [None] * CONV_GROUPS
                for k in range(K_B):
                    w_k = cb_w_ref[k, :, c0:c0 + LANES]
                    for u in range(CONV_GROUPS):
                        r = SUBLANES * (k + u) + r0
                        tap = w_k * b_perm[r:r + SUBLANES, c0:c0 + LANES]
                        accs[u] = tap if k == 0 else accs[u] + tap
                for u in range(CONV_GROUPS):
                    cb_buf[r0 + SUBLANES * u:r0 + SUBLANES * (u + 1), c0:c0 + LANES] = accs[u]
    new_b_ref[...] = b_ext[seg + tm - (K_B - 1):seg + tm, :]
    b_ext[0:seg, :] = b_ext[tm:tm + seg, :]
    conv_b = _interleave_segments(cb_buf[...], SUBLANES)
    b_n = _layer_norm(conv_b + cb_bias_ref[...], nb_g_ref[...], nb_b_ref[...])
    yb = _dot((b_n * _sigmoid(b_n)).astype(BF16), w_b_out_ref[...])

    za = _dot(xb, w_in_ref[:, 0:3 * d])
    a_in = za[:, d:2 * d] * za[:, 2 * d:3 * d]
    a_ext[A_PAD:A_PAD + tm, :] = a_in
    conv_a = (ca_w_ref[0:1, :] * a_ext[A_PAD - 2:A_PAD - 2 + tm, :]
              + ca_w_ref[1:2, :] * a_ext[A_PAD - 1:A_PAD - 1 + tm, :]
              + ca_w_ref[2:3, :] * a_in)
    new_a_ref[...] = a_ext[A_PAD + tm - (K_A - 1):A_PAD + tm, :]
    a_ext[0:A_PAD, :] = a_ext[tm:tm + A_PAD, :]
    ya = _dot((za[:, 0:d] * conv_a).astype(BF16), w_a_out_ref[...])

    zc = _dot(xb, w_in_ref[:, 5 * d:7 * d])
    v = _layer_norm(zc[:, d:2 * d], vn_g_ref[...], vn_b_ref[...]).astype(BF16)
    row = lax.broadcasted_iota(jnp.int32, (CHUNK, CHUNK), 0)
    col = lax.broadcasted_iota(jnp.int32, (CHUNK, CHUNK), 1)
    c_head = d // C_GROUPS
    for g in range(C_GROUPS):
        w_g = jnp.where(col <= row, w_sp_ref[g], 0.0).astype(BF16)
        bias_g = b_sp_t_ref[:, g:g + 1]
        for c in range(tm // CHUNK):
            s = _dot(w_g, v[c * CHUNK:(c + 1) * CHUNK, g * c_head:(g + 1) * c_head]) + bias_g
            c_buf[c * CHUNK:(c + 1) * CHUNK, g * c_head:(g + 1) * c_head] = s
    yc = _dot((zc[:, 0:d] * c_buf[...]).astype(BF16), w_c_out_ref[...])

    zg_all = _dot(xb, w_in_ref[:, 7 * d:10 * d])
    h = _merge_and_norm(x, zg_all, ya, yb, yc, w_o_ref, ln_g_ref, ln_b_ref, alpha)
    h_ref[...] = h
    _router_gates_t(h, rw_t_ref[...], rb_ref[...], gt_ref)


def _mixer_sample_kernel(x_ref, ctx_a_ref, ctx_b_ref, w_in_ref, ca_w_ref, w_a_out_ref, cb_w_ref, cb_bias_ref,
                         nb_g_ref, nb_b_ref, w_b_out_ref, vn_g_ref, vn_b_ref, sp_scale_ref, sp_bias_ref,
                         w_c_out_ref, w_o_ref, ln_g_ref, ln_b_ref, rw_t_ref, rb_ref,
                         h_ref, gt_ref, a_in_ref, b_glu_ref, v_ref, *, alpha):
    d = x_ref.shape[-1]
    x = x_ref[...]
    xb = x.astype(BF16)

    za = _dot(xb, w_in_ref[:, 0:3 * d])
    a_in = za[:, d:2 * d] * za[:, 2 * d:3 * d]
    conv_a = ca_w_ref[K_A - 1:K_A, :] * a_in
    for k in range(K_A - 1):
        conv_a = conv_a + ca_w_ref[k:k + 1, :] * ctx_a_ref[k]
    a_in_ref[...] = a_in
    ya = _dot((za[:, 0:d] * conv_a).astype(BF16), w_a_out_ref[...])

    zb = _dot(xb, w_in_ref[:, 3 * d:5 * d])
    b_glu = zb[:, 0:d] * _sigmoid(zb[:, d:2 * d])
    conv_b = cb_w_ref[K_B - 1:K_B, :] * b_glu
    for k in range(K_B - 1):
        conv_b = conv_b + cb_w_ref[k:k + 1, :] * ctx_b_ref[k]
    b_glu_ref[...] = b_glu
    b_n = _layer_norm(conv_b + cb_bias_ref[...], nb_g_ref[...], nb_b_ref[...])
    yb = _dot((b_n * _sigmoid(b_n)).astype(BF16), w_b_out_ref[...])

    zc = _dot(xb, w_in_ref[:, 5 * d:7 * d])
    v = _layer_norm(zc[:, d:2 * d], vn_g_ref[...], vn_b_ref[...])
    v_ref[...] = v
    yc = _dot((zc[:, 0:d] * (v * sp_scale_ref[...] + sp_bias_ref[...])).astype(BF16), w_c_out_ref[...])

    zg = _dot(xb, w_in_ref[:, 7 * d:10 * d])
    h = _merge_and_norm(x, zg, ya, yb, yc, w_o_ref, ln_g_ref, ln_b_ref, alpha)
    h_ref[...] = h
    _router_gates_t(h, rw_t_ref[...], rb_ref[...], gt_ref)


def _moe_dense_kernel(h_ref, g_ref, wg_ref, wu_ref, wd_ref, ln_g_ref, ln_b_ref, out_ref, acc_ref, hb_ref, *,
                      alpha):
    e = pl.program_id(1)

    @pl.when(e == 0)
    def _():
        hb_ref[...] = h_ref[...].astype(BF16)
        acc_ref[...] = jnp.zeros_like(acc_ref)

    hb = hb_ref[...]
    gate = _dot(hb, wg_ref[...].astype(BF16))
    u = gate * _sigmoid(gate) * _dot(hb, wu_ref[...].astype(BF16))
    y = _dot(u.astype(BF16), wd_ref[...].astype(BF16))
    g = g_ref[...]
    lane = lax.broadcasted_iota(jnp.int32, g.shape, 1)
    g_e = jnp.sum(jnp.where(lane == e, g, 0.0), axis=-1, keepdims=True)
    acc_ref[...] += g_e * y

    @pl.when(e == pl.num_programs(1) - 1)
    def _():
        out_ref[...] = _layer_norm(alpha * h_ref[...] + acc_ref[...], ln_g_ref[...], ln_b_ref[...])


def _resident(shape):
    zeros = (0,) * len(shape)
    return pl.BlockSpec(shape, lambda *_: zeros, pipeline_mode=pl.Buffered(1))


def _mixer_prompt(x, p, alpha):
    n_seq, seq, d = x.shape
    tm = MIX_TM
    seg = tm // SUBLANES
    assert seq % tm == 0 and tm % CHUNK == 0 and seg % SUBLANES == 0 and seg >= K_B - 1
    n_i = seq // tm
    n_tiles = n_seq * n_i
    weights = (p['w_in'], p['conv_a_w'], p['w_a_out'], p['conv_b_w8'], p['conv_b_bias'], p['norm_b_g'],
               p['norm_b_b'], p['w_b_out'], p['v_norm_g'], p['v_norm_b'], p['w_spatial'], p['b_spatial_t'],
               p['w_c_out'], p['w_o'], p['ln1_g'], p['ln1_b'], p['router_w_t'], p['router_b'])
    x2 = x.reshape(n_seq * seq, d)
    h, gt, new_a, new_b = pl.pallas_call(
        functools.partial(_mixer_prompt_kernel, alpha=alpha, tiles_per_seq=n_i),
        grid=(n_tiles,),
        in_specs=[pl.BlockSpec((tm, d), lambda n: (n, 0))] + [_resident(w.shape) for w in weights],
        out_specs=[pl.BlockSpec((tm, d), lambda n: (n, 0)),
                   pl.BlockSpec((N_EXPERTS, tm), lambda n: (0, n)),
                   pl.BlockSpec((None, K_A - 1, d), lambda n: (n // n_i, 0, 0)),
                   pl.BlockSpec((None, K_B - 1, d), lambda n: (n // n_i, 0, 0))],
        out_shape=[jax.ShapeDtypeStruct((n_seq * seq, d), F32),
                   jax.ShapeDtypeStruct((N_EXPERTS, n_seq * seq), F32),
                   jax.ShapeDtypeStruct((n_seq, K_A - 1, d), F32),
                   jax.ShapeDtypeStruct((n_seq, K_B - 1, d), F32)],
        scratch_shapes=[pltpu.VMEM((A_PAD + tm, d), F32), pltpu.VMEM((seg + tm, d), F32),
                        pltpu.VMEM(((K_B - 1) * SUBLANES + tm, d), F32),
                        pltpu.VMEM((tm, d), F32), pltpu.VMEM((tm, d), F32)],
        compiler_params=pltpu.CompilerParams(dimension_semantics=("arbitrary",),
                                             vmem_limit_bytes=VMEM_LIMIT),
        name="mixer_prompt",
    )(x2, *weights)
    return h, gt, new_a, new_b


def _mixer_sample(x, ctx_a_t, ctx_b_t, p, alpha):
    n, d = x.shape
    weights = (p['w_in'], p['conv_a_w'], p['w_a_out'], p['conv_b_w'], p['conv_b_bias'], p['norm_b_g'],
               p['norm_b_b'], p['w_b_out'], p['v_norm_g'], p['v_norm_b'], p['sp_scale'], p['sp_bias'],
               p['w_c_out'], p['w_o'], p['ln1_g'], p['ln1_b'], p['router_w_t'], p['router_b'])
    return pl.pallas_call(
        functools.partial(_mixer_sample_kernel, alpha=alpha),
        out_shape=[jax.ShapeDtypeStruct((n, d), F32),
                   jax.ShapeDtypeStruct((N_EXPERTS, n), F32),
                   jax.ShapeDtypeStruct((n, d), F32),
                   jax.ShapeDtypeStruct((n, d), F32),
                   jax.ShapeDtypeStruct((n, d), F32)],
        compiler_params=pltpu.CompilerParams(vmem_limit_bytes=VMEM_LIMIT),
        name="mixer_sample",
    )(x, ctx_a_t, ctx_b_t, *weights)


def _moe_dense(h, gates, p, alpha, tm, layer):
    t, d = h.shape
    assert t % tm == 0
    d_e = p['expert_w_gate'].shape[-1]
    return pl.pallas_call(
        functools.partial(_moe_dense_kernel, alpha=alpha),
        grid=(t // tm, N_EXPERTS),
        in_specs=[pl.BlockSpec((tm, d), lambda i, e: (i, 0)),
                  pl.BlockSpec((tm, N_EXPERTS), lambda i, e: (i, 0)),
                  pl.BlockSpec((None, None, d, d_e), lambda i, e: (layer, e, 0, 0)),
                  pl.BlockSpec((None, None, d, d_e), lambda i, e: (layer, e, 0, 0)),
                  pl.BlockSpec((None, None, d_e, d), lambda i, e: (layer, e, 0, 0)),
                  pl.BlockSpec((1, d), lambda i, e: (0, 0)),
                  pl.BlockSpec((1, d), lambda i, e: (0, 0))],
        out_specs=pl.BlockSpec((tm, d), lambda i, e: (i, 0)),
        out_shape=jax.ShapeDtypeStruct((t, d), F32),
        scratch_shapes=[pltpu.VMEM((tm, d), F32), pltpu.VMEM((tm, d), BF16)],
        compiler_params=pltpu.CompilerParams(dimension_semantics=("arbitrary", "arbitrary"),
                                             vmem_limit_bytes=VMEM_LIMIT),
        name="moe_dense",
    )(h, gates, p['expert_w_gate'], p['expert_w_up'], p['expert_w_down'], p['ln2_g'], p['ln2_b'])


def kernel(x_prompt, x_sample, state_conv_a, state_conv_b, w_in, conv_a_w, w_a_out, conv_b_w, conv_b_bias, norm_b_g, norm_b_b, w_b_out, v_norm_g, v_norm_b, w_spatial, b_spatial, w_c_out, w_o, ln1_g, ln1_b, router_w, router_b, expert_w_gate, expert_w_up, expert_w_down, ln2_g, ln2_b):
    depth = w_in.shape[0]
    n_seq, seq, d = x_prompt.shape
    n_dec, dec_seq, _ = x_sample.shape
    assert dec_seq == 1
    alpha = (2 * depth) ** 0.25
    c_head = d // C_GROUPS

    xp = x_prompt
    xs = x_sample.reshape(n_dec, d)
    router_w_t = router_w.T
    router_b_col = router_b.reshape(N_EXPERTS, 1)
    pa, pb, sa, sb, sv = [], [], [], [], []
    for l in range(depth):
        row = lambda a: a[l].reshape(1, d)
        p = dict(w_in=w_in[l].astype(BF16), conv_a_w=conv_a_w[l], w_a_out=w_a_out[l].astype(BF16),
                 conv_b_w=conv_b_w[l], conv_b_bias=row(conv_b_bias), norm_b_g=row(norm_b_g),
                 conv_b_w8=jnp.broadcast_to(conv_b_w[l][:, None, :], (K_B, SUBLANES, d)),
                 norm_b_b=row(norm_b_b), w_b_out=w_b_out[l].astype(BF16), v_norm_g=row(v_norm_g),
                 v_norm_b=row(v_norm_b), w_spatial=w_spatial[l], b_spatial_t=b_spatial[l].T,
                 sp_scale=jnp.repeat(w_spatial[l, :, 0, 0], c_head).reshape(1, d),
                 sp_bias=jnp.repeat(b_spatial[l, :, 0], c_head).reshape(1, d),
                 w_c_out=w_c_out[l].astype(BF16), w_o=w_o[l].astype(BF16), ln1_g=row(ln1_g), ln1_b=row(ln1_b),
                 router_w_t=router_w_t, router_b=router_b_col,
                 expert_w_gate=expert_w_gate, expert_w_up=expert_w_up, expert_w_down=expert_w_down,
                 ln2_g=row(ln2_g), ln2_b=row(ln2_b))

        hp, gtp, na, nb = _mixer_prompt(xp, p, alpha)
        pa.append(na)
        pb.append(nb)
        xp = _moe_dense(hp, gtp.T, p, alpha, MOE_TM, l).reshape(n_seq, seq, d)

        hs, gts, a_in, b_glu, v = _mixer_sample(xs, jnp.swapaxes(state_conv_a[l], 0, 1),
                                                jnp.swapaxes(state_conv_b[l], 0, 1), p, alpha)
        sa.append(jnp.concatenate([state_conv_a[l][:, 1:], a_in[:, None, :]], axis=1))
        sb.append(jnp.concatenate([state_conv_b[l][:, 1:], b_glu[:, None, :]], axis=1))
        sv.append(v.reshape(n_dec, 1, d))
        xs = _moe_dense(hs, gts.T, p, alpha, n_dec, l)

    return (xp, xs.reshape(n_dec, 1, d), jnp.stack(pa), jnp.stack(pb), jnp.stack(sa), jnp.stack(sb),
            jnp.stack(sv))
```

```python
import functools

import jax
import jax.numpy as jnp
from jax import lax
from jax.experimental import pallas as pl
from jax.experimental.pallas import tpu as pltpu

F32 = jnp.float32
BF16 = jnp.bfloat16

K_A = 3
K_B = 31
CHUNK = 128
C_GROUPS = 8
N_EXPERTS = 16
N_EXPERT_GROUPS = 4
EXPERTS_PER_GROUP = N_EXPERTS // N_EXPERT_GROUPS
LN_EPS = 1e-5

SUBLANES = 8
LANES = 128
A_PAD = SUBLANES
CONV_GROUPS = 16
B_TILE = 256
MIX_TM = 512
MOE_TM = 1024
VMEM_LIMIT = 58 * 1024 * 1024


def _sigmoid(x):
    return 1.0 / (1.0 + jnp.exp(-x))


def _layer_norm(x, g, b):
    mu = jnp.mean(x, axis=-1, keepdims=True)
    xc = x - mu
    var = jnp.mean(xc * xc, axis=-1, keepdims=True)
    return xc * lax.rsqrt(var + LN_EPS) * g + b


def _dot(a, b):
    return jnp.dot(a, b, preferred_element_type=F32)


def _interleave_segments(a, seg):
    n, d = a.shape
    return jnp.swapaxes(a.reshape(n // seg, seg, d), 0, 1).reshape(n, d)


def _first_max(vals):
    m = functools.reduce(jnp.maximum, vals)
    hits, taken = [], None
    for v in vals:
        hit = v == m
        if taken is None:
            taken = hit
        else:
            hit = jnp.logical_and(hit, jnp.logical_not(taken))
            taken = jnp.logical_or(taken, hit)
        hits.append(hit)
    return m, hits


def _router_gates_t(h, rw_t, rb, gt_ref):
    nt = (((1,), (1,)), ((), ()))
    h_hi = h.astype(BF16)
    h_lo = (h - h_hi.astype(F32)).astype(BF16)
    both = lax.dot_general(rw_t, h_hi, nt, preferred_element_type=F32)
    logits = (both[0:N_EXPERTS, :] + both[N_EXPERTS:, :]
              + lax.dot_general(rw_t[0:N_EXPERTS, :], h_lo, nt, preferred_element_type=F32) + rb)
    p = jnp.exp(logits - jnp.max(logits, axis=0, keepdims=True))
    rows = [p[e:e + 1, :] for e in range(N_EXPERTS)]
    scores, selected = [], []
    for g in range(N_EXPERT_GROUPS):
        a = rows[g * EXPERTS_PER_GROUP:(g + 1) * EXPERTS_PER_GROUP]
        m1, is1 = _first_max(a)
        rest = [jnp.where(hit, -1.0, v) for hit, v in zip(is1, a)]
        m2, is2 = _first_max(rest)
        scores.append(m1 + m2)
        selected.append([jnp.logical_or(x, y) for x, y in zip(is1, is2)])
    _, best = _first_max(scores)
    for g in range(N_EXPERT_GROUPS):
        inv = 1.0 / scores[g]
        for j in range(EXPERTS_PER_GROUP):
            e = g * EXPERTS_PER_GROUP + j
            keep = jnp.logical_and(best[g], selected[g][j])
            gt_ref[e:e + 1, :] = jnp.where(keep, rows[e] * inv, 0.0)


def _merge_and_norm(x, zg, ya, yb, yc, w_o_ref, ln_g_ref, ln_b_ref, alpha):
    d = x.shape[-1]
    merged = (_sigmoid(zg[:, :d]) * ya + _sigmoid(zg[:, d:2 * d]) * yb + _sigmoid(zg[:, 2 * d:]) * yc)
    m = _dot(merged.astype(BF16), w_o_ref[...])
    return _layer_norm(alpha * x + m, ln_g_ref[...], ln_b_ref[...])


def _mixer_prompt_kernel(x_ref, w_in_ref, ca_w_ref, w_a_out_ref, cb_w_ref, cb_bias_ref, nb_g_ref,
                         nb_b_ref, w_b_out_ref, vn_g_ref, vn_b_ref, w_sp_ref, b_sp_t_ref, w_c_out_ref, w_o_ref,
                         ln_g_ref, ln_b_ref, rw_t_ref, rb_ref,
                         h_ref, gt_ref, new_a_ref, new_b_ref,
                         a_ext, b_ext, b_perm, cb_buf, c_buf, *, alpha, tiles_per_seq):
    tm, d = x_ref.shape
    n = pl.program_id(0)

    @pl.when(n % tiles_per_seq == 0)
    def _():
        a_ext[0:A_PAD, :] = jnp.zeros((A_PAD, d), F32)
        b_ext[0:tm // SUBLANES, :] = jnp.zeros((tm // SUBLANES, d), F32)

    x = x_ref[...]
    xb = x.astype(BF16)

    seg = tm // SUBLANES
    halo = (K_B - 1) * SUBLANES
    for t0 in range(0, d, B_TILE):
        zv = _dot(xb, w_in_ref[:, 3 * d + t0:3 * d + t0 + B_TILE])
        zg = _dot(xb, w_in_ref[:, 4 * d + t0:4 * d + t0 + B_TILE])
        b_ext[seg:seg + tm, t0:t0 + B_TILE] = zv * _sigmoid(zg)
        b_perm[0:halo, t0:t0 + B_TILE] = _interleave_segments(b_ext[0:tm, t0:t0 + B_TILE], seg)[tm - halo:, :]
        b_perm[halo:halo + tm, t0:t0 + B_TILE] = _interleave_segments(b_ext[seg:seg + tm, t0:t0 + B_TILE], seg)
        for c0 in range(t0, t0 + B_TILE, LANES):
            for r0 in range(0, tm, CONV_GROUPS * SUBLANES):
                accs = [None] * CONV_GROUPS
                for k in range(K_B):
                    w_k = cb_w_ref[k, :, c0:c0 + LANES]
                    for u in range(CONV_GROUPS):
                        r = SUBLANES * (k + u) + r0
                        tap = w_k * b_perm[r:r + SUBLANES, c0:c0 + LANES]
                        accs[u] = tap if k == 0 else accs[u] + tap
                for u in range(CONV_GROUPS):
                    cb_buf[r0 + SUBLANES * u:r0 + SUBLANES * (u + 1), c0:c0 + LANES] = accs[u]
    new_b_ref[...] = b_ext[seg + tm - (K_B - 1):seg + tm, :]
    b_ext[0:seg, :] = b_ext[tm:tm + seg, :]
    conv_b = _interleave_segments(cb_buf[...], SUBLANES)
    b_n = _layer_norm(conv_b + cb_bias_ref[...], nb_g_ref[...], nb_b_ref[...])
    yb = _dot((b_n * _sigmoid(b_n)).astype(BF16), w_b_out_ref[...])

    za = _dot(xb, w_in_ref[:, 0:3 * d])
    a_in = za[:, d:2 * d] * za[:, 2 * d:3 * d]
    a_ext[A_PAD:A_PAD + tm, :] = a_in
    conv_a = (ca_w_ref[0:1, :] * a_ext[A_PAD - 2:A_PAD - 2 + tm, :]
              + ca_w_ref[1:2, :] * a_ext[A_PAD - 1:A_PAD - 1 + tm, :]
              + ca_w_ref[2:3, :] * a_in)
    new_a_ref[...] = a_ext[A_PAD + tm - (K_A - 1):A_PAD + tm, :]
    a_ext[0:A_PAD, :] = a_ext[tm:tm + A_PAD, :]
    ya = _dot((za[:, 0:d] * conv_a).astype(BF16), w_a_out_ref[...])

    zc = _dot(xb, w_in_ref[:, 5 * d:7 * d])
    v = _layer_norm(zc[:, d:2 * d], vn_g_ref[...], vn_b_ref[...]).astype(BF16)
    row = lax.broadcasted_iota(jnp.int32, (CHUNK, CHUNK), 0)
    col = lax.broadcasted_iota(jnp.int32, (CHUNK, CHUNK), 1)
    c_head = d // C_GROUPS
    for g in range(C_GROUPS):
        w_g = jnp.where(col <= row, w_sp_ref[g], 0.0).astype(BF16)
        bias_g = b_sp_t_ref[:, g:g + 1]
        for c in range(tm // CHUNK):
            s = _dot(w_g, v[c * CHUNK:(c + 1) * CHUNK, g * c_head:(g + 1) * c_head]) + bias_g
            c_buf[c * CHUNK:(c + 1) * CHUNK, g * c_head:(g + 1) * c_head] = s
    yc = _dot((zc[:, 0:d] * c_buf[...]).astype(BF16), w_c_out_ref[...])

    zg_all = _dot(xb, w_in_ref[:, 7 * d:10 * d])
    h = _merge_and_norm(x, zg_all, ya, yb, yc, w_o_ref, ln_g_ref, ln_b_ref, alpha)
    h_ref[...] = h
    _router_gates_t(h, rw_t_ref[...], rb_ref[...], gt_ref)


def _mixer_sample_kernel(x_ref, ctx_a_ref, ctx_b_ref, w_in_ref, ca_w_ref, w_a_out_ref, cb_w_ref, cb_bias_ref,
                         nb_g_ref, nb_b_ref, w_b_out_ref, vn_g_ref, vn_b_ref, sp_scale_ref, sp_bias_ref,
                         w_c_out_ref, w_o_ref, ln_g_ref, ln_b_ref, rw_t_ref, rb_ref,
                         h_ref, gt_ref, a_in_ref, b_glu_ref, v_ref, *, alpha):
    d = x_ref.shape[-1]
    x = x_ref[...]
    xb = x.astype(BF16)

    za = _dot(xb, w_in_ref[:, 0:3 * d])
    a_in = za[:, d:2 * d] * za[:, 2 * d:3 * d]
    conv_a = ca_w_ref[K_A - 1:K_A, :] * a_in
    for k in range(K_A - 1):
        conv_a = conv_a + ca_w_ref[k:k + 1, :] * ctx_a_ref[k]
    a_in_ref[...] = a_in
    ya = _dot((za[:, 0:d] * conv_a).astype(BF16), w_a_out_ref[...])

    zb = _dot(xb, w_in_ref[:, 3 * d:5 * d])
    b_glu = zb[:, 0:d] * _sigmoid(zb[:, d:2 * d])
    conv_b = cb_w_ref[K_B - 1:K_B, :] * b_glu
    for k in range(K_B - 1):
        conv_b = conv_b + cb_w_ref[k:k + 1, :] * ctx_b_ref[k]
    b_glu_ref[...] = b_glu
    b_n = _layer_norm(conv_b + cb_bias_ref[...], nb_g_ref[...], nb_b_ref[...])
    yb = _dot((b_n * _sigmoid(b_n)).astype(BF16), w_b_out_ref[...])

    zc = _dot(xb, w_in_ref[:, 5 * d:7 * d])
    v = _layer_norm(zc[:, d:2 * d], vn_g_ref[...], vn_b_ref[...])
    v_ref[...] = v
    yc = _dot((zc[:, 0:d] * (v * sp_scale_ref[...] + sp_bias_ref[...])).astype(BF16), w_c_out_ref[...])

    zg = _dot(xb, w_in_ref[:, 7 * d:10 * d])
    h = _merge_and_norm(x, zg, ya, yb, yc, w_o_ref, ln_g_ref, ln_b_ref, alpha)
    h_ref[...] = h
    _router_gates_t(h, rw_t_ref[...], rb_ref[...], gt_ref)


def _moe_dense_kernel(h_ref, g_ref, wg_ref, wu_ref, wd_ref, ln_g_ref, ln_b_ref, out_ref, acc_ref, hb_ref, *,
                      alpha):
    e = pl.program_id(1)

    @pl.when(e == 0)
    def _():
        hb_ref[...] = h_ref[...].astype(BF16)
        acc_ref[...] = jnp.zeros_like(acc_ref)

    hb = hb_ref[...]
    gate = _dot(hb, wg_ref[...].astype(BF16))
    u = gate * _sigmoid(gate) * _dot(hb, wu_ref[...].astype(BF16))
    y = _dot(u.astype(BF16), wd_ref[...].astype(BF16))
    g = g_ref[...]
    lane = lax.broadcasted_iota(jnp.int32, g.shape, 1)
    g_e = jnp.sum(jnp.where(lane == e, g, 0.0), axis=-1, keepdims=True)
    acc_ref[...] += g_e * y

    @pl.when(e == pl.num_programs(1) - 1)
    def _():
        out_ref[...] = _layer_norm(alpha * h_ref[...] + acc_ref[...], ln_g_ref[...], ln_b_ref[...])


def _resident(shape):
    zeros = (0,) * len(shape)
    return pl.BlockSpec(shape, lambda *_: zeros, pipeline_mode=pl.Buffered(1))


def _mixer_prompt(x, p, alpha):
    n_seq, seq, d = x.shape
    tm = MIX_TM
    seg = tm // SUBLANES
    assert seq % tm == 0 and tm % CHUNK == 0 and seg % SUBLANES == 0 and seg >= K_B - 1
    n_i = seq // tm
    n_tiles = n_seq * n_i
    weights = (p['w_in'], p['conv_a_w'], p['w_a_out'], p['conv_b_w8'], p['conv_b_bias'], p['norm_b_g'],
               p['norm_b_b'], p['w_b_out'], p['v_norm_g'], p['v_norm_b'], p['w_spatial'], p['b_spatial_t'],
               p['w_c_out'], p['w_o'], p['ln1_g'], p['ln1_b'], p['router_w_t'], p['router_b'])
    x2 = x.reshape(n_seq * seq, d)
    h, gt, new_a, new_b = pl.pallas_call(
        functools.partial(_mixer_prompt_kernel, alpha=alpha, tiles_per_seq=n_i),
        grid=(n_tiles,),
        in_specs=[pl.BlockSpec((tm, d), lambda n: (n, 0))] + [_resident(w.shape) for w in weights],
        out_specs=[pl.BlockSpec((tm, d), lambda n: (n, 0)),
                   pl.BlockSpec((N_EXPERTS, tm), lambda n: (0, n)),
                   pl.BlockSpec((None, K_A - 1, d), lambda n: (n // n_i, 0, 0)),
                   pl.BlockSpec((None, K_B - 1, d), lambda n: (n // n_i, 0, 0))],
        out_shape=[jax.ShapeDtypeStruct((n_seq * seq, d), F32),
                   jax.ShapeDtypeStruct((N_EXPERTS, n_seq * seq), F32),
                   jax.ShapeDtypeStruct((n_seq, K_A - 1, d), F32),
                   jax.ShapeDtypeStruct((n_seq, K_B - 1, d), F32)],
        scratch_shapes=[pltpu.VMEM((A_PAD + tm, d), F32), pltpu.VMEM((seg + tm, d), F32),
                        pltpu.VMEM(((K_B - 1) * SUBLANES + tm, d), F32),
                        pltpu.VMEM((tm, d), F32), pltpu.VMEM((tm, d), F32)],
        compiler_params=pltpu.CompilerParams(dimension_semantics=("arbitrary",),
                                             vmem_limit_bytes=VMEM_LIMIT),
        name="mixer_prompt",
    )(x2, *weights)
    return h, gt, new_a, new_b


def _mixer_sample(x, ctx_a_t, ctx_b_t, p, alpha):
    n, d = x.shape
    weights = (p['w_in'], p['conv_a_w'], p['w_a_out'], p['conv_b_w'], p['conv_b_bias'], p['norm_b_g'],
               p['norm_b_b'], p['w_b_out'], p['v_norm_g'], p['v_norm_b'], p['sp_scale'], p['sp_bias'],
               p['w_c_out'], p['w_o'], p['ln1_g'], p['ln1_b'], p['router_w_t'], p['router_b'])
    return pl.pallas_call(
        functools.partial(_mixer_sample_kernel, alpha=alpha),
        out_shape=[jax.ShapeDtypeStruct((n, d), F32),
                   jax.ShapeDtypeStruct((N_EXPERTS, n), F32),
                   jax.ShapeDtypeStruct((n, d), F32),
                   jax.ShapeDtypeStruct((n, d), F32),
                   jax.ShapeDtypeStruct((n, d), F32)],
        compiler_params=pltpu.CompilerParams(vmem_limit_bytes=VMEM_LIMIT),
        name="mixer_sample",
    )(x, ctx_a_t, ctx_b_t, *weights)


def _moe_dense(h, gates, p, alpha, tm, layer):
    t, d = h.shape
    assert t % tm == 0
    d_e = p['expert_w_gate'].shape[-1]
    return pl.pallas_call(
        functools.partial(_moe_dense_kernel, alpha=alpha),
        grid=(t // tm, N_EXPERTS),
        in_specs=[pl.BlockSpec((tm, d), lambda i, e: (i, 0)),
                  pl.BlockSpec((tm, N_EXPERTS), lambda i, e: (i, 0)),
                  pl.BlockSpec((None, None, d, d_e), lambda i, e: (layer, e, 0, 0)),
                  pl.BlockSpec((None, None, d, d_e), lambda i, e: (layer, e, 0, 0)),
                  pl.BlockSpec((None, None, d_e, d), lambda i, e: (layer, e, 0, 0)),
                  pl.BlockSpec((1, d), lambda i, e: (0, 0)),
                  pl.BlockSpec((1, d), lambda i, e: (0, 0))],
        out_specs=pl.BlockSpec((tm, d), lambda i, e: (i, 0)),
        out_shape=jax.ShapeDtypeStruct((t, d), F32),
        scratch_shapes=[pltpu.VMEM((tm, d), F32), pltpu.VMEM((tm, d), BF16)],
        compiler_params=pltpu.CompilerParams(dimension_semantics=("arbitrary", "arbitrary"),
                                             vmem_limit_bytes=VMEM_LIMIT),
        name="moe_dense",
    )(h, gates, p['expert_w_gate'], p['expert_w_up'], p['expert_w_down'], p['ln2_g'], p['ln2_b'])


def kernel(x_prompt, x_sample, state_conv_a, state_conv_b, w_in, conv_a_w, w_a_out, conv_b_w, conv_b_bias, norm_b_g, norm_b_b, w_b_out, v_norm_g, v_norm_b, w_spatial, b_spatial, w_c_out, w_o, ln1_g, ln1_b, router_w, router_b, expert_w_gate, expert_w_up, expert_w_down, ln2_g, ln2_b):
    depth = w_in.shape[0]
    n_seq, seq, d = x_prompt.shape
    n_dec, dec_seq, _ = x_sample.shape
    assert dec_seq == 1
    alpha = (2 * depth) ** 0.25
    c_head = d // C_GROUPS

    xp = x_prompt
    xs = x_sample.reshape(n_dec, d)
    router_w_hi = router_w.T.astype(BF16)
    router_w_t = jnp.concatenate([router_w_hi, (router_w.T - router_w_hi.astype(F32)).astype(BF16)], axis=0)
    router_b_col = router_b.reshape(N_EXPERTS, 1)
    pa, pb, sa, sb, sv = [], [], [], [], []
    for l in range(depth):
        row = lambda a: a[l].reshape(1, d)
        p = dict(w_in=w_in[l].astype(BF16), conv_a_w=conv_a_w[l], w_a_out=w_a_out[l].astype(BF16),
                 conv_b_w=conv_b_w[l], conv_b_bias=row(conv_b_bias), norm_b_g=row(norm_b_g),
                 conv_b_w8=jnp.broadcast_to(conv_b_w[l][:, None, :], (K_B, SUBLANES, d)),
                 norm_b_b=row(norm_b_b), w_b_out=w_b_out[l].astype(BF16), v_norm_g=row(v_norm_g),
                 v_norm_b=row(v_norm_b), w_spatial=w_spatial[l], b_spatial_t=b_spatial[l].T,
                 sp_scale=jnp.repeat(w_spatial[l, :, 0, 0], c_head).reshape(1, d),
                 sp_bias=jnp.repeat(b_spatial[l, :, 0], c_head).reshape(1, d),
                 w_c_out=w_c_out[l].astype(BF16), w_o=w_o[l].astype(BF16), ln1_g=row(ln1_g), ln1_b=row(ln1_b),
                 router_w_t=router_w_t, router_b=router_b_col,
                 expert_w_gate=expert_w_gate, expert_w_up=expert_w_up, expert_w_down=expert_w_down,
                 ln2_g=row(ln2_g), ln2_b=row(ln2_b))

        hp, gtp, na, nb = _mixer_prompt(xp, p, alpha)
        pa.append(na)
        pb.append(nb)
        xp = _moe_dense(hp, gtp.T, p, alpha, MOE_TM, l).reshape(n_seq, seq, d)

        hs, gts, a_in, b_glu, v = _mixer_sample(xs, jnp.swapaxes(state_conv_a[l], 0, 1),
                                                jnp.swapaxes(state_conv_b[l], 0, 1), p, alpha)
        sa.append(jnp.concatenate([state_conv_a[l][:, 1:], a_in[:, None, :]], axis=1))
        sb.append(jnp.concatenate([state_conv_b[l][:, 1:], b_glu[:, None, :]], axis=1))
        sv.append(v.reshape(n_dec, 1, d))
        xs = _moe_dense(hs, gts.T, p, alpha, n_dec, l)

    return (xp, xs.reshape(n_dec, 1, d), jnp.stack(pa), jnp.stack(pb), jnp.stack(sa), jnp.stack(sb),
            jnp.stack(sv))
```

```python
import functools

import jax
import jax.numpy as jnp
from jax import lax
from jax.experimental import pallas as pl
from jax.experimental.pallas import tpu as pltpu

F32 = jnp.float32
BF16 = jnp.bfloat16

K_A = 3
K_B = 31
CHUNK = 128
C_GROUPS = 8
N_EXPERTS = 16
N_EXPERT_GROUPS = 4
EXPERTS_PER_GROUP = N_EXPERTS // N_EXPERT_GROUPS
LN_EPS = 1e-5

SUBLANES = 8
LANES = 128
A_PAD = SUBLANES
CONV_GROUPS = 16
B_TILE = 256
PROJ_COLS = 512
CHUNKS_PER_PIECE = 3
MIX_TM = 512
MOE_TM = 1024
VMEM_LIMIT = 60 * 1024 * 1024


def _sigmoid(x):
    return 1.0 / (1.0 + jnp.exp(-x))


def _layer_norm(x, g, b):
    mu = jnp.mean(x, axis=-1, keepdims=True)
    xc = x - mu
    var = jnp.mean(xc * xc, axis=-1, keepdims=True)
    return xc * lax.rsqrt(var + LN_EPS) * g + b


def _dot(a, b):
    return jnp.dot(a, b, preferred_element_type=F32)


def _zero_after(x):
    bits = lax.bitcast_convert_type(x, jnp.uint32)
    return lax.bitcast_convert_type((bits >> 16) >> 16, F32)


def _interleave_segments(a, seg):
    n, d = a.shape
    return jnp.swapaxes(a.reshape(n // seg, seg, d), 0, 1).reshape(n, d)


def _first_max(vals):
    m = functools.reduce(jnp.maximum, vals)
    hits, taken = [], None
    for v in vals:
        hit = v == m
        if taken is None:
            taken = hit
        else:
            hit = jnp.logical_and(hit, jnp.logical_not(taken))
            taken = jnp.logical_or(taken, hit)
        hits.append(hit)
    return m, hits


def _router_gates_t(h, rw_t, rb, gt_ref):
    nt = (((1,), (1,)), ((), ()))
    h_hi = h.astype(BF16)
    h_lo = (h - h_hi.astype(F32)).astype(BF16)
    both = lax.dot_general(rw_t, h_hi, nt, preferred_element_type=F32)
    logits = (both[0:N_EXPERTS, :] + both[N_EXPERTS:, :]
              + lax.dot_general(rw_t[0:N_EXPERTS, :], h_lo, nt, preferred_element_type=F32) + rb)
    p = jnp.exp(logits - jnp.max(logits, axis=0, keepdims=True))
    rows = [p[e:e + 1, :] for e in range(N_EXPERTS)]
    scores, selected = [], []
    for g in range(N_EXPERT_GROUPS):
        a = rows[g * EXPERTS_PER_GROUP:(g + 1) * EXPERTS_PER_GROUP]
        m1, is1 = _first_max(a)
        rest = [jnp.where(hit, -1.0, v) for hit, v in zip(is1, a)]
        m2, is2 = _first_max(rest)
        scores.append(m1 + m2)
        selected.append([jnp.logical_or(x, y) for x, y in zip(is1, is2)])
    _, best = _first_max(scores)
    for g in range(N_EXPERT_GROUPS):
        inv = 1.0 / scores[g]
        for j in range(EXPERTS_PER_GROUP):
            e = g * EXPERTS_PER_GROUP + j
            keep = jnp.logical_and(best[g], selected[g][j])
            gt_ref[e:e + 1, :] = jnp.where(keep, rows[e] * inv, 0.0)


def _merge_and_norm(x, zg, ya, yb, yc, w_o_ref, ln_g_ref, ln_b_ref, alpha):
    d = x.shape[-1]
    merged = (_sigmoid(zg[:, :d]) * ya + _sigmoid(zg[:, d:2 * d]) * yb + _sigmoid(zg[:, 2 * d:]) * yc)
    m = _dot(merged.astype(BF16), w_o_ref[...])
    return _layer_norm(alpha * x + m, ln_g_ref[...], ln_b_ref[...])


def _mixer_prompt_kernel(x_ref, w_in_ref, ca_w_ref, w_a_out_ref, cb_w_ref, cb_bias_ref, nb_g_ref,
                         nb_b_ref, w_b_out_ref, vn_g_ref, vn_b_ref, w_sp_ref, b_sp_t_ref, w_c_out_ref, w_o_ref,
                         ln_g_ref, ln_b_ref, rw_t_ref, rb_ref,
                         h_ref, gt_ref, new_a_ref, new_b_ref,
                         za_scr, zc_scr, a_ext, b_ext, b_perm, cb_buf, *, alpha, tiles_per_seq):
    tm, d = x_ref.shape
    n = pl.program_id(0)

    @pl.when(n % tiles_per_seq == 0)
    def _():
        a_ext[0:A_PAD, :] = jnp.zeros((A_PAD, d), F32)
        b_ext[0:tm // SUBLANES, :] = jnp.zeros((tm // SUBLANES, d), F32)

    x = x_ref[...]
    xb = x.astype(BF16)

    seg = tm // SUBLANES
    halo = (K_B - 1) * SUBLANES
    a_pieces = ([(za_scr, 0, c) for c in range(0, 3 * d, PROJ_COLS)]
                + [(zc_scr, 5 * d, c) for c in range(0, 2 * d, PROJ_COLS)])
    chunk = 0
    gate_zero = None
    for t0 in range(0, d, B_TILE):
        zv = _dot(xb, w_in_ref[:, 3 * d + t0:3 * d + t0 + B_TILE])
        zg = _dot(xb, w_in_ref[:, 4 * d + t0:4 * d + t0 + B_TILE])
        b_ext[seg:seg + tm, t0:t0 + B_TILE] = zv * _sigmoid(zg)
        b_perm[0:halo, t0:t0 + B_TILE] = _interleave_segments(b_ext[0:tm, t0:t0 + B_TILE], seg)[tm - halo:, :]
        b_perm[halo:halo + tm, t0:t0 + B_TILE] = _interleave_segments(b_ext[seg:seg + tm, t0:t0 + B_TILE], seg)
        for c0 in range(t0, t0 + B_TILE, LANES):
            for r0 in range(0, tm, CONV_GROUPS * SUBLANES):
                accs = [None] * CONV_GROUPS
                for k in range(K_B):
                    w_k = cb_w_ref[k, :, c0:c0 + LANES]
                    for u in range(CONV_GROUPS):
                        r = SUBLANES * (k + u) + r0
                        tap = w_k * b_perm[r:r + SUBLANES, c0:c0 + LANES]
                        if k == 0 and u == 0 and gate_zero is not None:
                            tap = tap + gate_zero
                            gate_zero = None
                        accs[u] = tap if k == 0 else accs[u] + tap
                for u in range(CONV_GROUPS):
                    cb_buf[r0 + SUBLANES * u:r0 + SUBLANES * (u + 1), c0:c0 + LANES] = accs[u]
                if chunk % CHUNKS_PER_PIECE == 0 and a_pieces:
                    z_scr, base, p0 = a_pieces.pop(0)
                    piece = _dot(xb, w_in_ref[:, base + p0:base + p0 + PROJ_COLS])
                    z_scr[:, p0:p0 + PROJ_COLS] = piece
                    gate_zero = _zero_after(piece[0:SUBLANES, 0:LANES])
                chunk += 1
    for z_scr, base, p0 in a_pieces:
        z_scr[:, p0:p0 + PROJ_COLS] = _dot(xb, w_in_ref[:, base + p0:base + p0 + PROJ_COLS])
    new_b_ref[...] = b_ext[seg + tm - (K_B - 1):seg + tm, :]
    b_ext[0:seg, :] = b_ext[tm:tm + seg, :]
    conv_b = _interleave_segments(cb_buf[...], SUBLANES)
    b_n = _layer_norm(conv_b + cb_bias_ref[...], nb_g_ref[...], nb_b_ref[...])
    yb = _dot((b_n * _sigmoid(b_n)).astype(BF16), w_b_out_ref[...])

    za = za_scr[...]
    a_in = za[:, d:2 * d] * za[:, 2 * d:3 * d]
    a_ext[A_PAD:A_PAD + tm, :] = a_in
    conv_a = (ca_w_ref[0:1, :] * a_ext[A_PAD - 2:A_PAD - 2 + tm, :]
              + ca_w_ref[1:2, :] * a_ext[A_PAD - 1:A_PAD - 1 + tm, :]
              + ca_w_ref[2:3, :] * a_in)
    new_a_ref[...] = a_ext[A_PAD + tm - (K_A - 1):A_PAD + tm, :]
    a_ext[0:A_PAD, :] = a_ext[tm:tm + A_PAD, :]
    ya = _dot((za[:, 0:d] * conv_a).astype(BF16), w_a_out_ref[...])

    zc = zc_scr[...]
    c_buf = cb_buf
    v = _layer_norm(zc[:, d:2 * d], vn_g_ref[...], vn_b_ref[...]).astype(BF16)
    row = lax.broadcasted_iota(jnp.int32, (CHUNK, CHUNK), 0)
    col = lax.broadcasted_iota(jnp.int32, (CHUNK, CHUNK), 1)
    c_head = d // C_GROUPS
    for g in range(C_GROUPS):
        w_g = jnp.where(col <= row, w_sp_ref[g], 0.0).astype(BF16)
        bias_g = b_sp_t_ref[:, g:g + 1]
        for c in range(tm // CHUNK):
            s = _dot(w_g, v[c * CHUNK:(c + 1) * CHUNK, g * c_head:(g + 1) * c_head]) + bias_g
            c_buf[c * CHUNK:(c + 1) * CHUNK, g * c_head:(g + 1) * c_head] = s
    yc = _dot((zc[:, 0:d] * c_buf[...]).astype(BF16), w_c_out_ref[...])

    zg_all = _dot(xb, w_in_ref[:, 7 * d:10 * d])
    h = _merge_and_norm(x, zg_all, ya, yb, yc, w_o_ref, ln_g_ref, ln_b_ref, alpha)
    h_ref[...] = h
    _router_gates_t(h, rw_t_ref[...], rb_ref[...], gt_ref)


def _mixer_sample_kernel(x_ref, ctx_a_ref, ctx_b_ref, w_in_ref, ca_w_ref, w_a_out_ref, cb_w_ref, cb_bias_ref,
                         nb_g_ref, nb_b_ref, w_b_out_ref, vn_g_ref, vn_b_ref, sp_scale_ref, sp_bias_ref,
                         w_c_out_ref, w_o_ref, ln_g_ref, ln_b_ref, rw_t_ref, rb_ref,
                         h_ref, gt_ref, a_in_ref, b_glu_ref, v_ref, *, alpha):
    d = x_ref.shape[-1]
    x = x_ref[...]
    xb = x.astype(BF16)

    za = _dot(xb, w_in_ref[:, 0:3 * d])
    a_in = za[:, d:2 * d] * za[:, 2 * d:3 * d]
    conv_a = ca_w_ref[K_A - 1:K_A, :] * a_in
    for k in range(K_A - 1):
        conv_a = conv_a + ca_w_ref[k:k + 1, :] * ctx_a_ref[k]
    a_in_ref[...] = a_in
    ya = _dot((za[:, 0:d] * conv_a).astype(BF16), w_a_out_ref[...])

    zb = _dot(xb, w_in_ref[:, 3 * d:5 * d])
    b_glu = zb[:, 0:d] * _sigmoid(zb[:, d:2 * d])
    conv_b = cb_w_ref[K_B - 1:K_B, :] * b_glu
    for k in range(K_B - 1):
        conv_b = conv_b + cb_w_ref[k:k + 1, :] * ctx_b_ref[k]
    b_glu_ref[...] = b_glu
    b_n = _layer_norm(conv_b + cb_bias_ref[...], nb_g_ref[...], nb_b_ref[...])
    yb = _dot((b_n * _sigmoid(b_n)).astype(BF16), w_b_out_ref[...])

    zc = _dot(xb, w_in_ref[:, 5 * d:7 * d])
    v = _layer_norm(zc[:, d:2 * d], vn_g_ref[...], vn_b_ref[...])
    v_ref[...] = v
    yc = _dot((zc[:, 0:d] * (v * sp_scale_ref[...] + sp_bias_ref[...])).astype(BF16), w_c_out_ref[...])

    zg = _dot(xb, w_in_ref[:, 7 * d:10 * d])
    h = _merge_and_norm(x, zg, ya, yb, yc, w_o_ref, ln_g_ref, ln_b_ref, alpha)
    h_ref[...] = h
    _router_gates_t(h, rw_t_ref[...], rb_ref[...], gt_ref)


def _moe_dense_kernel(h_ref, g_ref, wg_ref, wu_ref, wd_ref, ln_g_ref, ln_b_ref, out_ref, acc_ref, hb_ref, *,
                      alpha):
    e = pl.program_id(1)

    @pl.when(e == 0)
    def _():
        hb_ref[...] = h_ref[...].astype(BF16)
        acc_ref[...] = jnp.zeros_like(acc_ref)

    hb = hb_ref[...]
    gate = _dot(hb, wg_ref[...].astype(BF16))
    u = gate * _sigmoid(gate) * _dot(hb, wu_ref[...].astype(BF16))
    y = _dot(u.astype(BF16), wd_ref[...].astype(BF16))
    g = g_ref[...]
    lane = lax.broadcasted_iota(jnp.int32, g.shape, 1)
    g_e = jnp.sum(jnp.where(lane == e, g, 0.0), axis=-1, keepdims=True)
    acc_ref[...] += g_e * y

    @pl.when(e == pl.num_programs(1) - 1)
    def _():
        out_ref[...] = _layer_norm(alpha * h_ref[...] + acc_ref[...], ln_g_ref[...], ln_b_ref[...])


def _resident(shape):
    zeros = (0,) * len(shape)
    return pl.BlockSpec(shape, lambda *_: zeros, pipeline_mode=pl.Buffered(1))


def _mixer_prompt(x, p, alpha):
    n_seq, seq, d = x.shape
    tm = MIX_TM
    seg = tm // SUBLANES
    assert seq % tm == 0 and tm % CHUNK == 0 and seg % SUBLANES == 0 and seg >= K_B - 1
    n_i = seq // tm
    n_tiles = n_seq * n_i
    weights = (p['w_in'], p['conv_a_w'], p['w_a_out'], p['conv_b_w8'], p['conv_b_bias'], p['norm_b_g'],
               p['norm_b_b'], p['w_b_out'], p['v_norm_g'], p['v_norm_b'], p['w_spatial'], p['b_spatial_t'],
               p['w_c_out'], p['w_o'], p['ln1_g'], p['ln1_b'], p['router_w_t'], p['router_b'])
    x2 = x.reshape(n_seq * seq, d)
    h, gt, new_a, new_b = pl.pallas_call(
        functools.partial(_mixer_prompt_kernel, alpha=alpha, tiles_per_seq=n_i),
        grid=(n_tiles,),
        in_specs=[pl.BlockSpec((tm, d), lambda n: (n, 0))] + [_resident(w.shape) for w in weights],
        out_specs=[pl.BlockSpec((tm, d), lambda n: (n, 0)),
                   pl.BlockSpec((N_EXPERTS, tm), lambda n: (0, n)),
                   pl.BlockSpec((None, K_A - 1, d), lambda n: (n // n_i, 0, 0)),
                   pl.BlockSpec((None, K_B - 1, d), lambda n: (n // n_i, 0, 0))],
        out_shape=[jax.ShapeDtypeStruct((n_seq * seq, d), F32),
                   jax.ShapeDtypeStruct((N_EXPERTS, n_seq * seq), F32),
                   jax.ShapeDtypeStruct((n_seq, K_A - 1, d), F32),
                   jax.ShapeDtypeStruct((n_seq, K_B - 1, d), F32)],
        scratch_shapes=[pltpu.VMEM((tm, 3 * d), F32), pltpu.VMEM((tm, 2 * d), F32),
                        pltpu.VMEM((A_PAD + tm, d), F32), pltpu.VMEM((seg + tm, d), F32),
                        pltpu.VMEM(((K_B - 1) * SUBLANES + tm, d), F32),
                        pltpu.VMEM((tm, d), F32)],
        compiler_params=pltpu.CompilerParams(dimension_semantics=("arbitrary",),
                                             vmem_limit_bytes=VMEM_LIMIT),
        name="mixer_prompt",
    )(x2, *weights)
    return h, gt, new_a, new_b


def _mixer_sample(x, ctx_a_t, ctx_b_t, p, alpha):
    n, d = x.shape
    weights = (p['w_in'], p['conv_a_w'], p['w_a_out'], p['conv_b_w'], p['conv_b_bias'], p['norm_b_g'],
               p['norm_b_b'], p['w_b_out'], p['v_norm_g'], p['v_norm_b'], p['sp_scale'], p['sp_bias'],
               p['w_c_out'], p['w_o'], p['ln1_g'], p['ln1_b'], p['router_w_t'], p['router_b'])
    return pl.pallas_call(
        functools.partial(_mixer_sample_kernel, alpha=alpha),
        out_shape=[jax.ShapeDtypeStruct((n, d), F32),
                   jax.ShapeDtypeStruct((N_EXPERTS, n), F32),
                   jax.ShapeDtypeStruct((n, d), F32),
                   jax.ShapeDtypeStruct((n, d), F32),
                   jax.ShapeDtypeStruct((n, d), F32)],
        compiler_params=pltpu.CompilerParams(vmem_limit_bytes=VMEM_LIMIT),
        name="mixer_sample",
    )(x, ctx_a_t, ctx_b_t, *weights)


def _moe_dense(h, gates, p, alpha, tm, layer):
    t, d = h.shape
    assert t % tm == 0
    d_e = p['expert_w_gate'].shape[-1]
    return pl.pallas_call(
        functools.partial(_moe_dense_kernel, alpha=alpha),
        grid=(t // tm, N_EXPERTS),
        in_specs=[pl.BlockSpec((tm, d), lambda i, e: (i, 0)),
                  pl.BlockSpec((tm, N_EXPERTS), lambda i, e: (i, 0)),
                  pl.BlockSpec((None, None, d, d_e), lambda i, e: (layer, e, 0, 0)),
                  pl.BlockSpec((None, None, d, d_e), lambda i, e: (layer, e, 0, 0)),
                  pl.BlockSpec((None, None, d_e, d), lambda i, e: (layer, e, 0, 0)),
                  pl.BlockSpec((1, d), lambda i, e: (0, 0)),
                  pl.BlockSpec((1, d), lambda i, e: (0, 0))],
        out_specs=pl.BlockSpec((tm, d), lambda i, e: (i, 0)),
        out_shape=jax.ShapeDtypeStruct((t, d), F32),
        scratch_shapes=[pltpu.VMEM((tm, d), F32), pltpu.VMEM((tm, d), BF16)],
        compiler_params=pltpu.CompilerParams(dimension_semantics=("arbitrary", "arbitrary"),
                                             vmem_limit_bytes=VMEM_LIMIT),
        name="moe_dense",
    )(h, gates, p['expert_w_gate'], p['expert_w_up'], p['expert_w_down'], p['ln2_g'], p['ln2_b'])


def kernel(x_prompt, x_sample, state_conv_a, state_conv_b, w_in, conv_a_w, w_a_out, conv_b_w, conv_b_bias, norm_b_g, norm_b_b, w_b_out, v_norm_g, v_norm_b, w_spatial, b_spatial, w_c_out, w_o, ln1_g, ln1_b, router_w, router_b, expert_w_gate, expert_w_up, expert_w_down, ln2_g, ln2_b):
    depth = w_in.shape[0]
    n_seq, seq, d = x_prompt.shape
    n_dec, dec_seq, _ = x_sample.shape
    assert dec_seq == 1
    alpha = (2 * depth) ** 0.25
    c_head = d // C_GROUPS

    xp = x_prompt
    xs = x_sample.reshape(n_dec, d)
    router_w_hi = router_w.T.astype(BF16)
    router_w_t = jnp.concatenate([router_w_hi, (router_w.T - router_w_hi.astype(F32)).astype(BF16)], axis=0)
    router_b_col = router_b.reshape(N_EXPERTS, 1)
    pa, pb, sa, sb, sv = [], [], [], [], []
    for l in range(depth):
        row = lambda a: a[l].reshape(1, d)
        p = dict(w_in=w_in[l].astype(BF16), conv_a_w=conv_a_w[l], w_a_out=w_a_out[l].astype(BF16),
                 conv_b_w=conv_b_w[l], conv_b_bias=row(conv_b_bias), norm_b_g=row(norm_b_g),
                 conv_b_w8=jnp.broadcast_to(conv_b_w[l][:, None, :], (K_B, SUBLANES, d)),
                 norm_b_b=row(norm_b_b), w_b_out=w_b_out[l].astype(BF16), v_norm_g=row(v_norm_g),
                 v_norm_b=row(v_norm_b), w_spatial=w_spatial[l], b_spatial_t=b_spatial[l].T,
                 sp_scale=jnp.repeat(w_spatial[l, :, 0, 0], c_head).reshape(1, d),
                 sp_bias=jnp.repeat(b_spatial[l, :, 0], c_head).reshape(1, d),
                 w_c_out=w_c_out[l].astype(BF16), w_o=w_o[l].astype(BF16), ln1_g=row(ln1_g), ln1_b=row(ln1_b),
                 router_w_t=router_w_t, router_b=router_b_col,
                 expert_w_gate=expert_w_gate, expert_w_up=expert_w_up, expert_w_down=expert_w_down,
                 ln2_g=row(ln2_g), ln2_b=row(ln2_b))

        hp, gtp, na, nb = _mixer_prompt(xp, p, alpha)
        pa.append(na)
        pb.append(nb)
        xp = _moe_dense(hp, gtp.T, p, alpha, MOE_TM, l).reshape(n_seq, seq, d)

        hs, gts, a_in, b_glu, v = _mixer_sample(xs, jnp.swapaxes(state_conv_a[l], 0, 1),
                                                jnp.swapaxes(state_conv_b[l], 0, 1), p, alpha)
        sa.append(jnp.concatenate([state_conv_a[l][:, 1:], a_in[:, None, :]], axis=1))
        sb.append(jnp.concatenate([state_conv_b[l][:, 1:], b_glu[:, None, :]], axis=1))
        sv.append(v.reshape(n_dec, 1, d))
        xs = _moe_dense(hs, gts.T, p, alpha, n_dec, l)

    return (xp, xs.reshape(n_dec, 1, d), jnp.stack(pa), jnp.stack(pb), jnp.stack(sa), jnp.stack(sb),
            jnp.stack(sv))
```
